```python
import math
import jax, jax.numpy as jnp
from jax import lax
import numpy as np

D_MODEL = 1024
BATCH = 8
SEQ = 8192
DEPTH = 2
DEC_BATCH = 4
DEC_SEQ = 4096
PAST_LEN = 128

GRID_W = 64
N_MIXERS = 2
N_POOL_LAYERS = (DEPTH + 1) // 2
N_NA_LAYERS = DEPTH // 2
POOL_EXPAND = 2
POOL_WIDTH = POOL_EXPAND * D_MODEL
POOL_WINDOWS = (2, 4, 8, 16)
N_POOL_GROUPS = len(POOL_WINDOWS)
POOL_GROUP_W = POOL_WIDTH // N_POOL_GROUPS
NA_WIDTH = D_MODEL
NA_HEAD_DIM = 32
NA_HEADS = NA_WIDTH // NA_HEAD_DIM
WIN_H = 8
WIN_W = 16
NA_SCALE = NA_HEAD_DIM ** -0.5
LN_EPS = 1e-5
DEEPNORM_ALPHA = (2 * DEPTH) ** 0.25
DEEPNORM_BETA = (8 * DEPTH) ** -0.25

kernel_name = 'hybrid_pool_natten_deepnorm_encoder'


def layer_norm(x, g, b):
    x32 = x.astype(jnp.float32)
    mu = jnp.mean(x32, axis=-1, keepdims=True)
    var = jnp.mean(jnp.square(x32 - mu), axis=-1, keepdims=True)
    y = (x32 - mu) * lax.rsqrt(var + LN_EPS) * g.astype(jnp.float32) + b.astype(jnp.float32)
    return y.astype(x.dtype)


def centred_mean_minus_self(u, w):
    seq = u.shape[1]
    h = w // 2
    u32 = u.astype(jnp.float32)
    cs = jnp.pad(jnp.cumsum(u32, axis=1), ((0, 0), (1, 0), (0, 0)))
    ext = jnp.pad(cs, ((0, 0), (h, h), (0, 0)), mode='edge')
    t = jnp.arange(seq)
    cnt = (jnp.minimum(t + h, seq) - jnp.maximum(t - h, 0)).astype(jnp.float32)
    mean = (ext[:, w:w + seq] - ext[:, :seq]) / cnt[None, :, None]
    return (mean - u32).astype(u.dtype)


def pool_branch(x, w_in, w_grp, scale, w_out):
    bsz, seq, _ = x.shape
    u, gate = jnp.split(x @ w_in, 2, axis=-1)
    ug = u.reshape(bsz, seq, N_POOL_GROUPS, POOL_GROUP_W)
    mixed = jnp.stack([centred_mean_minus_self(ug[:, :, i], POOL_WINDOWS[i])
                       for i in range(N_POOL_GROUPS)], axis=2)
    y = jnp.einsum('bsgc,gcd->bsgd', mixed, w_grp).reshape(bsz, seq, POOL_WIDTH) * scale
    return (y * jax.nn.silu(gate)) @ w_out


def neighbourhood_attention(q, k, v, rpb):
    bsz, seq = q.shape[0], q.shape[1]
    rows = seq // GRID_W
    kh = min(WIN_H, rows)
    row_start = np.clip(np.arange(rows) - kh // 2, 0, rows - kh)
    col_start = np.clip(np.arange(GRID_W) - WIN_W // 2, 0, GRID_W - WIN_W)
    col_idx = col_start[:, None] + np.arange(WIN_W)[None, :]
    dc = col_idx - np.arange(GRID_W)[:, None] + (WIN_W - 1)
    dr = row_start[:, None] + np.arange(kh)[None, :] - np.arange(rows)[:, None] + (WIN_H - 1)
    grid = lambda t: t.reshape(bsz, rows, GRID_W, NA_HEADS, NA_HEAD_DIM)
    qg, kg, vg = grid(q), grid(k), grid(v)
    rpb32 = rpb.astype(jnp.float32)

    def row_block(args):
        q_r, r0, dr_r = args
        k_w = lax.dynamic_slice_in_dim(kg, r0, kh, axis=1)[:, :, col_idx]
        v_w = lax.dynamic_slice_in_dim(vg, r0, kh, axis=1)[:, :, col_idx]
        s = jnp.einsum('bqhd,biqjhd->bhqij', q_r, k_w).astype(jnp.float32) * NA_SCALE
        bias = jnp.take(rpb32, dr_r, axis=1)[:, :, dc]
        s = s + jnp.transpose(bias, (0, 2, 1, 3))[None]
        p = jax.nn.softmax(s.reshape(bsz, NA_HEADS, GRID_W, kh * WIN_W), axis=-1).reshape(s.shape)
        return jnp.einsum('bhqij,biqjhd->bqhd', p.astype(v_w.dtype), v_w)

    out = lax.map(row_block, (jnp.moveaxis(qg, 1, 0),
                              jnp.asarray(row_start, dtype=jnp.int32),
                              jnp.asarray(dr, dtype=jnp.int32)))
    return jnp.moveaxis(out, 0, 1).reshape(bsz, seq, NA_WIDTH)


def na_branch(x, w_in, rpb, w_out):
    bsz, seq, _ = x.shape
    q, k, v, gate = jnp.split(x @ w_in, 4, axis=-1)
    heads = lambda t: t.reshape(bsz, seq, NA_HEADS, NA_HEAD_DIM)
    o = neighbourhood_attention(heads(q), heads(k), heads(v), rpb)
    return (o * jax.nn.silu(gate)) @ w_out


def trunk(x, w_in_pool, w_grp_pool, scale_pool, w_out_pool, w_in_na, rpb_na, w_out_na, ln_g, ln_b):
    for i in range(DEPTH):
        j = i // N_MIXERS
        if i % N_MIXERS == 0:
            h = pool_branch(x, w_in_pool[j], w_grp_pool[j], scale_pool[j], w_out_pool[j])
        else:
            h = na_branch(x, w_in_na[j], rpb_na[j], w_out_na[j])
        x = layer_norm(DEEPNORM_ALPHA * x + h, ln_g[i], ln_b[i])
    return x


def setup_inputs(seed: int = 0) -> dict:
    key = jax.random.key(seed)
    ks = jax.random.split(key, 11)
    f32 = jnp.float32
    nrm = lambda k, shape: jax.random.normal(k, shape, dtype=f32)
    return {
        'x_prompt': nrm(ks[0], (BATCH, SEQ, D_MODEL)),
        'x_sample': nrm(ks[1], (DEC_BATCH, DEC_SEQ, D_MODEL)),
        'w_in_pool': nrm(ks[2], (N_POOL_LAYERS, D_MODEL, 2 * POOL_WIDTH)) * D_MODEL ** -0.5,
        'w_grp_pool': nrm(ks[3], (N_POOL_LAYERS, N_POOL_GROUPS, POOL_GROUP_W, POOL_GROUP_W)) * POOL_GROUP_W ** -0.5,
        'scale_pool': 1.0 + 0.02 * nrm(ks[4], (N_POOL_LAYERS, POOL_WIDTH)),
        'w_out_pool': nrm(ks[5], (N_POOL_LAYERS, POOL_WIDTH, D_MODEL)) * (POOL_WIDTH ** -0.5 * DEEPNORM_BETA),
        'w_in_na': nrm(ks[6], (N_NA_LAYERS, D_MODEL, 4 * NA_WIDTH)) * D_MODEL ** -0.5,
        'rpb_na': 0.1 * nrm(ks[7], (N_NA_LAYERS, NA_HEADS, 2 * WIN_H - 1, 2 * WIN_W - 1)),
        'w_out_na': nrm(ks[8], (N_NA_LAYERS, NA_WIDTH, D_MODEL)) * (NA_WIDTH ** -0.5 * DEEPNORM_BETA),
        'ln_g': 1.0 + 0.02 * nrm(ks[9], (DEPTH, D_MODEL)),
        'ln_b': 0.02 * nrm(ks[10], (DEPTH, D_MODEL)),
    }


def reference(x_prompt, x_sample, w_in_pool, w_grp_pool, scale_pool, w_out_pool,
              w_in_na, rpb_na, w_out_na, ln_g, ln_b):
    y_prompt = trunk(x_prompt, w_in_pool, w_grp_pool, scale_pool, w_out_pool,
                     w_in_na, rpb_na, w_out_na, ln_g, ln_b)
    y_sample = trunk(x_sample, w_in_pool, w_grp_pool, scale_pool, w_out_pool,
                     w_in_na, rpb_na, w_out_na, ln_g, ln_b)
    return (y_prompt, y_sample)
```

```python
import functools

import jax
import jax.numpy as jnp
import numpy as np
from jax import lax
from jax.experimental import pallas as pl
from jax.experimental.pallas import tpu as pltpu

F32 = jnp.float32
BF16 = jnp.bfloat16

GRID_W = 64
POOL_WINDOWS = (2, 4, 8, 16)
HEAD_DIM = 32
WIN_H = 8
WIN_W = 16
NA_SCALE = HEAD_DIM ** -0.5
LN_EPS = 1e-5
DEPTH = 2
DEEPNORM_ALPHA = (2 * DEPTH) ** 0.25

V7X_SUBLANES_F32 = 8
V7X_LANES = 128
V7X_MXU_DIM = 256
V7X_VMEM_BYTES = 64 * 1024 * 1024

MASK_VALUE = -1e30
POOL_HALO = max(POOL_WINDOWS) // 2
HEADS_PER_CHUNK = V7X_MXU_DIM // HEAD_DIM
ROWS_PER_STEP = 8
HALO_ROWS = WIN_H // 2


def _silu(x):
    return x / (1.0 + jnp.exp(-x))


def _layer_norm(z, g, b):
    mu = jnp.mean(z, axis=-1, keepdims=True)
    zc = z - mu
    var = jnp.mean(zc * zc, axis=-1, keepdims=True)
    return zc * lax.rsqrt(var + LN_EPS) * g + b


def _pool_layer_kernel(xp_ref, x_ref, xn_ref, win_ref, wgrp_ref, scale_ref, wout_ref, g_ref, b_ref,
                       o_ref, xe_ref, xm_ref, *, seq, ts):
    j = pl.program_id(1)
    nj = pl.num_programs(1)
    x = x_ref[0]
    prev = jnp.where(j > 0, xp_ref[0], 0.0)
    nxt = jnp.where(j < nj - 1, xn_ref[0], 0.0)
    xe_ref[...] = jnp.concatenate([prev, x, nxt], axis=0).astype(BF16)
    xm_ref[...] = x.astype(BF16)

    n_ext = ts + 2 * POOL_HALO
    t = j * ts + lax.broadcasted_iota(jnp.int32, (ts, 1), 0)
    n_groups = len(POOL_WINDOWS)
    gw = wgrp_ref.shape[-1]
    width = n_groups * gw
    acc = jnp.zeros(o_ref.shape[1:], F32)
    for g, w in enumerate(POOL_WINDOWS):
        h = w // 2
        u = jnp.dot(xe_ref[...], win_ref[:, g * gw:(g + 1) * gw], preferred_element_type=F32)
        gate = jnp.dot(xm_ref[...], win_ref[:, width + g * gw:width + (g + 1) * gw],
                       preferred_element_type=F32)
        b = u + pltpu.roll(u, 1, 0)
        span = 2
        while span < h:
            b = b + pltpu.roll(b, span, 0)
            span *= 2
        if h == 1:
            wsum = b
        else:
            wsum = pltpu.roll(b, n_ext - (h - 1), 0) + pltpu.roll(b, 1, 0)
        cnt = (jnp.minimum(t + h, seq) - jnp.maximum(t - h, 0)).astype(F32)
        mixed = wsum[POOL_HALO:POOL_HALO + ts] * (1.0 / cnt) - u[POOL_HALO:POOL_HALO + ts]
        y = jnp.dot(mixed.astype(BF16), wgrp_ref[g], preferred_element_type=F32)
        y = y * scale_ref[:, g * gw:(g + 1) * gw]
        a = (y * _silu(gate)).astype(BF16)
        acc = acc + jnp.dot(a, wout_ref[g * gw:(g + 1) * gw, :], preferred_element_type=F32)
    z = DEEPNORM_ALPHA * x + acc
    o_ref[0] = _layer_norm(z, g_ref[...], b_ref[...])


def _resident(shape):
    nd = len(shape)
    return pl.BlockSpec(shape, lambda *_: (0,) * nd, pipeline_mode=pl.Buffered(1))


def _pool_layer(x, w_in, w_grp, scale, w_out, ln_g, ln_b, *, ts=512):
    bsz, seq, d = x.shape
    assert seq % ts == 0 and ts % POOL_HALO == 0
    n_halo_blocks = seq // POOL_HALO
    per_tile = ts // POOL_HALO
    kernel = functools.partial(_pool_layer_kernel, seq=seq, ts=ts)
    return pl.pallas_call(
        kernel,
        out_shape=jax.ShapeDtypeStruct(x.shape, F32),
        grid=(bsz, seq // ts),
        in_specs=[
            pl.BlockSpec((1, POOL_HALO, d), lambda b, j: (b, jnp.maximum(j * per_tile - 1, 0), 0)),
            pl.BlockSpec((1, ts, d), lambda b, j: (b, j, 0)),
            pl.BlockSpec((1, POOL_HALO, d), lambda b, j: (b, jnp.minimum((j + 1) * per_tile, n_halo_blocks - 1), 0)),
            _resident(w_in.shape), _resident(w_grp.shape), _resident(scale.shape), _resident(w_out.shape),
            _resident(ln_g.shape), _resident(ln_b.shape),
        ],
        out_specs=pl.BlockSpec((1, ts, d), lambda b, j: (b, j, 0)),
        scratch_shapes=[pltpu.VMEM((ts + 2 * POOL_HALO, d), BF16), pltpu.VMEM((ts, d), BF16)],
        compiler_params=pltpu.CompilerParams(
            dimension_semantics=("arbitrary", "arbitrary"),
            vmem_limit_bytes=48 * 1024 * 1024),
        name="pool_layer",
    )(x, x, x, w_in, w_grp, scale, w_out, ln_g, ln_b)


def _na_inproj_kernel(x_ref, w_ref, q_ref, k_ref, v_ref, gate_ref):
    xb = x_ref[0].astype(BF16)
    width = gate_ref.shape[-1]
    n_chunks = q_ref.shape[1]
    cw = q_ref.shape[-1]
    for part, ref in enumerate((q_ref, k_ref, v_ref)):
        r = jnp.dot(xb, w_ref[:, part * width:(part + 1) * width], preferred_element_type=F32)
        if part == 0:
            r = r * NA_SCALE
        r = r.astype(BF16)
        for c in range(n_chunks):
            ref[0, c] = r[:, c * cw:(c + 1) * cw]
    gate_ref[0] = jnp.dot(xb, w_ref[:, 3 * width:4 * width], preferred_element_type=F32)


def _na_inproj(x, w_in, *, tm=512):
    bsz, seq, d = x.shape
    width = w_in.shape[1] // 4
    n_chunks = width // V7X_MXU_DIM
    chunked = jax.ShapeDtypeStruct((bsz, n_chunks, seq, V7X_MXU_DIM), BF16)
    chunk_spec = pl.BlockSpec((1, n_chunks, tm, V7X_MXU_DIM), lambda b, j: (b, 0, j, 0))
    return pl.pallas_call(
        _na_inproj_kernel,
        out_shape=(chunked, chunked, chunked, jax.ShapeDtypeStruct((bsz, seq, width), F32)),
        grid=(bsz, seq // tm),
        in_specs=[pl.BlockSpec((1, tm, d), lambda b, j: (b, j, 0)), _resident(w_in.shape)],
        out_specs=(chunk_spec, chunk_spec, chunk_spec, pl.BlockSpec((1, tm, width), lambda b, j: (b, j, 0))),
        compiler_params=pltpu.CompilerParams(
            dimension_semantics=("arbitrary", "arbitrary"),
            vmem_limit_bytes=40 * 1024 * 1024),
        name="na_inproj",
    )(x, w_in)


def _na_layer_kernel(q_ref, kp_ref, k_ref, kn_ref, vp_ref, v_ref, vn_ref, gate_ref, x_ref, tab_ref,
                     wout_ref, g_ref, b_ref, o_ref, kbuf, vbuf, obuf, *, rows):
    blk = pl.program_id(1)
    n_chunks = q_ref.shape[1]
    cw = q_ref.shape[-1]
    halo = HALO_ROWS * GRID_W
    main = ROWS_PER_STEP * GRID_W
    n_keys = WIN_H * GRID_W
    for c in range(n_chunks):
        kbuf[c, 0:halo] = kp_ref[0, c]
        kbuf[c, halo:halo + main] = k_ref[0, c]
        kbuf[c, halo + main:] = kn_ref[0, c]
        vbuf[c, 0:halo] = vp_ref[0, c]
        vbuf[c, halo:halo + main] = v_ref[0, c]
        vbuf[c, halo + main:] = vn_ref[0, c]

    lane_head = lax.broadcasted_iota(jnp.int32, (GRID_W, cw), 1) // HEAD_DIM
    head_masks = [lane_head == h for h in range(HEADS_PER_CHUNK)]
    lanes_per_block = 2 * GRID_W

    def row_body(i, carry):
        r = blk * ROWS_PER_STEP + i
        rs = jnp.clip(r - HALO_ROWS, 0, rows - WIN_H)
        rs_local = rs - (blk * ROWS_PER_STEP - HALO_ROWS)
        row_off = r - rs
        koff = pl.multiple_of(rs_local * GRID_W, GRID_W)
        qoff = pl.multiple_of(i * GRID_W, GRID_W)

        def chunk_body(c, carry2):
            kc = kbuf[c, pl.ds(koff, n_keys), :]
            vc = vbuf[c, pl.ds(koff, n_keys), :]
            qc = q_ref[0, c, pl.ds(qoff, GRID_W), :]
            zero = jnp.zeros_like(qc)
            qs = jnp.concatenate([jnp.where(m, qc, zero) for m in head_masks], axis=0)
            s = lax.dot_general(qs, kc, (((1,), (1,)), ((), ())), preferred_element_type=F32)
            biased = []
            for h in range(HEADS_PER_CHUNK):
                bias = jnp.concatenate(
                    [tab_ref[c * HEADS_PER_CHUNK + h, (WIN_H - 1) - row_off + 2 * m]
                     for m in range(n_keys // lanes_per_block)], axis=1)
                biased.append(s[h * GRID_W:(h + 1) * GRID_W] + bias)
            s = jnp.concatenate(biased, axis=0)
            p = jnp.exp(s - jnp.max(s, axis=-1, keepdims=True))
            l = jnp.sum(p, axis=-1, keepdims=True)
            pv = jnp.dot(p.astype(BF16), vc, preferred_element_type=F32) * (1.0 / l)
            oc = jnp.zeros((GRID_W, cw), F32)
            for h in range(HEADS_PER_CHUNK):
                oc = oc + jnp.where(head_masks[h], pv[h * GRID_W:(h + 1) * GRID_W], 0.0)
            obuf[c, pl.ds(qoff, GRID_W), :] = oc
            return carry2

        lax.fori_loop(0, n_chunks, chunk_body, 0)
        return carry

    lax.fori_loop(0, ROWS_PER_STEP, row_body, 0)

    o_all = jnp.concatenate([obuf[c] for c in range(n_chunks)], axis=1)
    a = (o_all * _silu(gate_ref[0])).astype(BF16)
    h2 = jnp.dot(a, wout_ref[...], preferred_element_type=F32)
    z = DEEPNORM_ALPHA * x_ref[0] + h2
    o_ref[0] = _layer_norm(z, g_ref[...], b_ref[...])


def _bias_table(rpb):
    n_pairs = 2 * WIN_H - 2
    lane = np.arange(2 * GRID_W)
    kc = lane % GRID_W
    dr = np.arange(n_pairs)[:, None] + (lane // GRID_W)[None, :]
    qcol = np.arange(GRID_W)[:, None]
    cs = np.clip(qcol - WIN_W // 2, 0, GRID_W - WIN_W)
    valid = (kc[None, :] >= cs) & (kc[None, :] < cs + WIN_W)
    dc = np.clip(kc[None, :] - qcol + (WIN_W - 1), 0, 2 * WIN_W - 2)
    tab = rpb.astype(F32)[:, dr[:, None, :], dc[None, :, :]]
    return jnp.where(jnp.asarray(valid)[None, None], tab, MASK_VALUE)


def _na_layer(q, k, v, gate, x, tab, w_out, ln_g, ln_b):
    bsz, seq, d = x.shape
    rows = seq // GRID_W
    n_chunks, cw = q.shape[1], q.shape[3]
    main = ROWS_PER_STEP * GRID_W
    halo = HALO_ROWS * GRID_W
    assert rows % ROWS_PER_STEP == 0 and rows >= WIN_H and main % halo == 0
    per_main = main // halo
    n_halo_blocks = seq // halo
    main_spec = pl.BlockSpec((1, n_chunks, main, cw), lambda b, j: (b, 0, j, 0))
    prev_spec = pl.BlockSpec((1, n_chunks, halo, cw), lambda b, j: (b, 0, jnp.maximum(j * per_main - 1, 0), 0))
    next_spec = pl.BlockSpec((1, n_chunks, halo, cw),
                             lambda b, j: (b, 0, jnp.minimum((j + 1) * per_main, n_halo_blocks - 1), 0))
    tok_spec = pl.BlockSpec((1, main, d), lambda b, j: (b, j, 0))
    kernel = functools.partial(_na_layer_kernel, rows=rows)
    return pl.pallas_call(
        kernel,
        out_shape=jax.ShapeDtypeStruct(x.shape, F32),
        grid=(bsz, rows // ROWS_PER_STEP),
        in_specs=[main_spec, prev_spec, main_spec, next_spec, prev_spec, main_spec, next_spec,
                  tok_spec, tok_spec, _resident(tab.shape), _resident(w_out.shape),
                  _resident(ln_g.shape), _resident(ln_b.shape)],
        out_specs=tok_spec,
        scratch_shapes=[pltpu.VMEM((n_chunks, main + 2 * halo, cw), BF16),
                        pltpu.VMEM((n_chunks, main + 2 * halo, cw), BF16),
                        pltpu.VMEM((n_chunks, main, cw), F32)],
        compiler_params=pltpu.CompilerParams(
            dimension_semantics=("arbitrary", "arbitrary"),
            vmem_limit_bytes=60 * 1024 * 1024),
        name="na_layer",
    )(q, k, k, k, v, v, v, gate, x, tab, w_out, ln_g, ln_b)


def _trunk(x, p):
    x1 = _pool_layer(x, p["w_in_pool"], p["w_grp"], p["scale"], p["w_out_pool"], p["ln_g0"], p["ln_b0"])
    q, k, v, gate = _na_inproj(x1, p["w_in_na"])
    return _na_layer(q, k, v, gate, x1, p["tab"], p["w_out_na"], p["ln_g1"], p["ln_b1"])


def kernel(x_prompt, x_sample, w_in_pool, w_grp_pool, scale_pool, w_out_pool, w_in_na, rpb_na, w_out_na, ln_g, ln_b):
    p = {
        "w_in_pool": w_in_pool[0].astype(BF16),
        "w_grp": w_grp_pool[0].astype(BF16),
        "scale": scale_pool[0].reshape(1, -1).astype(F32),
        "w_out_pool": w_out_pool[0].astype(BF16),
        "w_in_na": w_in_na[0].astype(BF16),
        "w_out_na": w_out_na[0].astype(BF16),
        "tab": _bias_table(rpb_na[0]),
        "ln_g0": ln_g[0].reshape(1, -1).astype(F32),
        "ln_b0": ln_b[0].reshape(1, -1).astype(F32),
        "ln_g1": ln_g[1].reshape(1, -1).astype(F32),
        "ln_b1": ln_b[1].reshape(1, -1).astype(F32),
    }
    return (_trunk(x_prompt, p), _trunk(x_sample, p))
```

```python
import functools

import jax
import jax.numpy as jnp
import numpy as np
from jax import lax
from jax.experimental import pallas as pl
from jax.experimental.pallas import tpu as pltpu

F32 = jnp.float32
BF16 = jnp.bfloat16

GRID_W = 64
POOL_WINDOWS = (2, 4, 8, 16)
HEAD_DIM = 32
WIN_H = 8
WIN_W = 16
NA_SCALE = HEAD_DIM ** -0.5
LN_EPS = 1e-5
DEPTH = 2
DEEPNORM_ALPHA = (2 * DEPTH) ** 0.25

V7X_SUBLANES_F32 = 8
V7X_LANES = 128
V7X_MXU_DIM = 256
V7X_VMEM_BYTES = 64 * 1024 * 1024

MASK_VALUE = -1e30
POOL_HALO = max(POOL_WINDOWS) // 2
HEADS_PER_CHUNK = V7X_MXU_DIM // HEAD_DIM
ROWS_PER_STEP = 8
HALO_ROWS = WIN_H // 2
V7X_SUBLANES_BF16 = 16
KEY_COLS = 2 * WIN_W
KEY_COL_ALIGN = V7X_SUBLANES_BF16
KEY_ROWS_PER_BLOCK = V7X_LANES // KEY_COLS


def _query_groups():
    groups = []
    for q in range(GRID_W):
        cs = min(max(q - WIN_W // 2, 0), GRID_W - WIN_W)
        k0 = min(cs // KEY_COL_ALIGN * KEY_COL_ALIGN, GRID_W - KEY_COLS)
        assert k0 <= cs and cs + WIN_W <= k0 + KEY_COLS
        if groups and groups[-1][2] == k0:
            groups[-1][1] = q + 1
        else:
            groups.append([q, q + 1, k0])
    assert all(g[0] % V7X_SUBLANES_F32 == 0 for g in groups)
    return tuple(tuple(g) for g in groups)


QUERY_GROUPS = _query_groups()
SCORE_LOOKAHEAD = 6


def _silu(x):
    return x / (1.0 + jnp.exp(-x))


def _layer_norm(z, g, b):
    mu = jnp.mean(z, axis=-1, keepdims=True)
    zc = z - mu
    var = jnp.mean(zc * zc, axis=-1, keepdims=True)
    return zc * lax.rsqrt(var + LN_EPS) * g + b


def _pool_layer_kernel(xp_ref, x_ref, xn_ref, win_ref, wgrp_ref, scale_ref, wout_ref, g_ref, b_ref,
                       o_ref, xe_ref, xm_ref, *, seq, ts):
    j = pl.program_id(1)
    nj = pl.num_programs(1)
    x = x_ref[0]
    prev = jnp.where(j > 0, xp_ref[0], 0.0)
    nxt = jnp.where(j < nj - 1, xn_ref[0], 0.0)
    xe_ref[...] = jnp.concatenate([prev, x, nxt], axis=0).astype(BF16)
    xm_ref[...] = x.astype(BF16)

    n_ext = ts + 2 * POOL_HALO
    t = j * ts + lax.broadcasted_iota(jnp.int32, (ts, 1), 0)
    n_groups = len(POOL_WINDOWS)
    gw = wgrp_ref.shape[-1]
    width = n_groups * gw
    acc = jnp.zeros(o_ref.shape[1:], F32)
    for g, w in enumerate(POOL_WINDOWS):
        h = w // 2
        u = jnp.dot(xe_ref[...], win_ref[:, g * gw:(g + 1) * gw], preferred_element_type=F32)
        gate = jnp.dot(xm_ref[...], win_ref[:, width + g * gw:width + (g + 1) * gw],
                       preferred_element_type=F32)
        b = u + pltpu.roll(u, 1, 0)
        span = 2
        while span < h:
            b = b + pltpu.roll(b, span, 0)
            span *= 2
        if h == 1:
            wsum = b
        else:
            wsum = pltpu.roll(b, n_ext - (h - 1), 0) + pltpu.roll(b, 1, 0)
        cnt = (jnp.minimum(t + h, seq) - jnp.maximum(t - h, 0)).astype(F32)
        mixed = wsum[POOL_HALO:POOL_HALO + ts] * (1.0 / cnt) - u[POOL_HALO:POOL_HALO + ts]
        y = jnp.dot(mixed.astype(BF16), wgrp_ref[g], preferred_element_type=F32)
        y = y * scale_ref[:, g * gw:(g + 1) * gw]
        a = (y * _silu(gate)).astype(BF16)
        acc = acc + jnp.dot(a, wout_ref[g * gw:(g + 1) * gw, :], preferred_element_type=F32)
    z = DEEPNORM_ALPHA * x + acc
    o_ref[0] = _layer_norm(z, g_ref[...], b_ref[...])


def _resident(shape):
    nd = len(shape)
    return pl.BlockSpec(shape, lambda *_: (0,) * nd, pipeline_mode=pl.Buffered(1))


def _pool_layer(x, w_in, w_grp, scale, w_out, ln_g, ln_b, *, ts=512):
    bsz, seq, d = x.shape
    assert seq % ts == 0 and ts % POOL_HALO == 0
    n_halo_blocks = seq // POOL_HALO
    per_tile = ts // POOL_HALO
    kernel = functools.partial(_pool_layer_kernel, seq=seq, ts=ts)
    return pl.pallas_call(
        kernel,
        out_shape=jax.ShapeDtypeStruct(x.shape, F32),
        grid=(bsz, seq // ts),
        in_specs=[
            pl.BlockSpec((1, POOL_HALO, d), lambda b, j: (b, jnp.maximum(j * per_tile - 1, 0), 0)),
            pl.BlockSpec((1, ts, d), lambda b, j: (b, j, 0)),
            pl.BlockSpec((1, POOL_HALO, d), lambda b, j: (b, jnp.minimum((j + 1) * per_tile, n_halo_blocks - 1), 0)),
            _resident(w_in.shape), _resident(w_grp.shape), _resident(scale.shape), _resident(w_out.shape),
            _resident(ln_g.shape), _resident(ln_b.shape),
        ],
        out_specs=pl.BlockSpec((1, ts, d), lambda b, j: (b, j, 0)),
        scratch_shapes=[pltpu.VMEM((ts + 2 * POOL_HALO, d), BF16), pltpu.VMEM((ts, d), BF16)],
        compiler_params=pltpu.CompilerParams(
            dimension_semantics=("arbitrary", "arbitrary"),
            vmem_limit_bytes=48 * 1024 * 1024),
        name="pool_layer",
    )(x, x, x, w_in, w_grp, scale, w_out, ln_g, ln_b)


def _na_inproj_kernel(x_ref, w_ref, q_ref, k_ref, v_ref, gate_ref):
    xb = x_ref[0].astype(BF16)
    width = gate_ref.shape[-1]
    n_chunks = q_ref.shape[1]
    cw = q_ref.shape[-1]
    for part, ref in enumerate((q_ref, k_ref, v_ref)):
        r = jnp.dot(xb, w_ref[:, part * width:(part + 1) * width], preferred_element_type=F32)
        if part == 0:
            r = r * NA_SCALE
        r = r.astype(BF16)
        for c in range(n_chunks):
            ref[0, c] = r[:, c * cw:(c + 1) * cw]
    gate_ref[0] = jnp.dot(xb, w_ref[:, 3 * width:4 * width], preferred_element_type=F32)


def _na_inproj(x, w_in, *, tm=512):
    bsz, seq, d = x.shape
    width = w_in.shape[1] // 4
    n_chunks = width // V7X_MXU_DIM
    chunked = jax.ShapeDtypeStruct((bsz, n_chunks, seq, V7X_MXU_DIM), BF16)
    chunk_spec = pl.BlockSpec((1, n_chunks, tm, V7X_MXU_DIM), lambda b, j: (b, 0, j, 0))
    return pl.pallas_call(
        _na_inproj_kernel,
        out_shape=(chunked, chunked, chunked, jax.ShapeDtypeStruct((bsz, seq, width), F32)),
        grid=(bsz, seq // tm),
        in_specs=[pl.BlockSpec((1, tm, d), lambda b, j: (b, j, 0)), _resident(w_in.shape)],
        out_specs=(chunk_spec, chunk_spec, chunk_spec, pl.BlockSpec((1, tm, width), lambda b, j: (b, j, 0))),
        compiler_params=pltpu.CompilerParams(
            dimension_semantics=("arbitrary", "arbitrary"),
            vmem_limit_bytes=40 * 1024 * 1024),
        name="na_inproj",
    )(x, w_in)


def _na_layer_kernel(q_ref, kp_ref, k_ref, kn_ref, vp_ref, v_ref, vn_ref, gate_ref, x_ref, tab_ref,
                     wout_ref, g_ref, b_ref, o_ref, kbuf, vbuf, obuf, *, rows):
    blk = pl.program_id(1)
    n_chunks = q_ref.shape[1]
    cw = q_ref.shape[-1]
    halo = HALO_ROWS * GRID_W
    main = ROWS_PER_STEP * GRID_W
    for c in range(n_chunks):
        kbuf[c, 0:halo] = kp_ref[0, c]
        kbuf[c, halo:halo + main] = k_ref[0, c]
        kbuf[c, halo + main:] = kn_ref[0, c]
        vbuf[c, 0:halo] = vp_ref[0, c]
        vbuf[c, halo:halo + main] = v_ref[0, c]
        vbuf[c, halo + main:] = vn_ref[0, c]

    def head_masks_for(n_rows):
        lane_head = lax.broadcasted_iota(jnp.int32, (n_rows, cw), 1) // HEAD_DIM
        return [lane_head == h for h in range(HEADS_PER_CHUNK)]

    head_masks = {nq: head_masks_for(nq) for nq in sorted({q1 - q0 for q0, q1, _ in QUERY_GROUPS})}
    n_blocks = WIN_H // KEY_ROWS_PER_BLOCK

    def row_body(i, carry):
        r = blk * ROWS_PER_STEP + i
        rs = jnp.clip(r - HALO_ROWS, 0, rows - WIN_H)
        rs_local = rs - (blk * ROWS_PER_STEP - HALO_ROWS)
        row_off = r - rs
        koff = rs_local * GRID_W
        qoff = pl.multiple_of(i * GRID_W, GRID_W)

        def scores(c, group):
            q0, q1, k0 = group
            nq = q1 - q0
            qf = q_ref[0, c, pl.ds(qoff, GRID_W), :].astype(F32)
            starts = [pl.multiple_of(koff + kr * GRID_W + k0, KEY_COL_ALIGN) for kr in range(WIN_H)]
            kg = jnp.concatenate([kbuf[c, pl.ds(s0, KEY_COLS), :] for s0 in starts], axis=0)
            qs = jnp.concatenate([jnp.where(m, qf[q0:q1], 0.0) for m in head_masks[nq]],
                                 axis=0).astype(BF16)
            return lax.dot_general(qs, kg, (((1,), (1,)), ((), ())), preferred_element_type=F32), starts

        def finish(c, group, s, starts):
            q0, q1, _ = group
            nq = q1 - q0
            masks = head_masks[nq]
            biased = []
            for h in range(HEADS_PER_CHUNK):
                bias = jnp.concatenate(
                    [tab_ref[c * HEADS_PER_CHUNK + h, (WIN_H - 1) - row_off + KEY_ROWS_PER_BLOCK * m, q0:q1, :]
                     for m in range(n_blocks)], axis=1)
                biased.append(s[h * nq:(h + 1) * nq] + bias)
            s = jnp.concatenate(biased, axis=0)
            p = jnp.exp(s - jnp.max(s, axis=-1, keepdims=True))
            l = jnp.sum(p, axis=-1, keepdims=True)
            vg = jnp.concatenate([vbuf[c, pl.ds(s0, KEY_COLS), :] for s0 in starts], axis=0)
            pv = jnp.dot(p.astype(BF16), vg, preferred_element_type=F32)
            oc = jnp.zeros((nq, cw), F32)
            lc = jnp.zeros((nq, cw), F32)
            for h in range(HEADS_PER_CHUNK):
                oc = oc + jnp.where(masks[h], pv[h * nq:(h + 1) * nq], 0.0)
                lc = lc + jnp.where(masks[h], l[h * nq:(h + 1) * nq], 0.0)
            obuf[c, pl.ds(pl.multiple_of(qoff + q0, V7X_SUBLANES_F32), nq), :] = oc * (1.0 / lc)

        units = [(c, g) for c in range(n_chunks) for g in QUERY_GROUPS]
        pending = []
        for unit in units:
            pending.append((unit, scores(*unit)))
            if len(pending) > SCORE_LOOKAHEAD:
                (c, g), (s, starts) = pending.pop(0)
                finish(c, g, s, starts)
        for (c, g), (s, starts) in pending:
            finish(c, g, s, starts)
        return carry

    lax.fori_loop(0, ROWS_PER_STEP, row_body, 0)

    o_all = jnp.concatenate([obuf[c] for c in range(n_chunks)], axis=1)
    a = (o_all * _silu(gate_ref[0])).astype(BF16)
    h2 = jnp.dot(a, wout_ref[...], preferred_element_type=F32)
    z = DEEPNORM_ALPHA * x_ref[0] + h2
    o_ref[0] = _layer_norm(z, g_ref[...], b_ref[...])


def _bias_table_kernel(y_ref, o_ref):
    q = lax.broadcasted_iota(jnp.int32, (GRID_W, V7X_LANES), 0)
    kcc = lax.broadcasted_iota(jnp.int32, (GRID_W, V7X_LANES), 1) % KEY_COLS
    cs = jnp.clip(q - WIN_W // 2, 0, GRID_W - WIN_W)
    k0 = jnp.zeros_like(q)
    for q0, _, first_key in QUERY_GROUPS:
        k0 = jnp.where(q >= q0, first_key, k0)
    kc = k0 + kcc
    valid = (kc >= cs) & (kc < cs + WIN_W)
    for d in range(o_ref.shape[1]):
        yb = jnp.broadcast_to(y_ref[0, d:d + 1, :], (GRID_W, V7X_LANES))
        t = jnp.zeros((GRID_W, V7X_LANES), F32)
        for q0, _, first_key in QUERY_GROUPS:
            shift = (-(first_key + WIN_W - 1)) % V7X_LANES
            rolled = pltpu.roll(yb, shift, 1, stride=1, stride_axis=0)
            t = jnp.where(q >= q0, rolled, t)
        o_ref[0, d] = jnp.where(valid, t, MASK_VALUE)


def _bias_table(rpb):
    n_heads = rpb.shape[0]
    n_d = 2 * WIN_H - KEY_ROWS_PER_BLOCK
    padded = jnp.pad(rpb.astype(F32), ((0, 0), (0, 0), (0, KEY_COLS - rpb.shape[2])))
    y = jnp.stack([padded[:, d:d + KEY_ROWS_PER_BLOCK].reshape(n_heads, V7X_LANES) for d in range(n_d)], axis=1)
    return pl.pallas_call(
        _bias_table_kernel,
        out_shape=jax.ShapeDtypeStruct((n_heads, n_d, GRID_W, V7X_LANES), F32),
        grid=(n_heads,),
        in_specs=[pl.BlockSpec((1, n_d, V7X_LANES), lambda h: (h, 0, 0))],
        out_specs=pl.BlockSpec((1, n_d, GRID_W, V7X_LANES), lambda h: (h, 0, 0, 0)),
        compiler_params=pltpu.CompilerParams(dimension_semantics=("arbitrary",)),
        name="bias_table",
    )(y)


def _na_layer(q, k, v, gate, x, tab, w_out, ln_g, ln_b):
    bsz, seq, d = x.shape
    rows = seq // GRID_W
    n_chunks, cw = q.shape[1], q.shape[3]
    main = ROWS_PER_STEP * GRID_W
    halo = HALO_ROWS * GRID_W
    assert rows % ROWS_PER_STEP == 0 and rows >= WIN_H and main % halo == 0
    per_main = main // halo
    n_halo_blocks = seq // halo
    main_spec = pl.BlockSpec((1, n_chunks, main, cw), lambda b, j: (b, 0, j, 0))
    prev_spec = pl.BlockSpec((1, n_chunks, halo, cw), lambda b, j: (b, 0, jnp.maximum(j * per_main - 1, 0), 0))
    next_spec = pl.BlockSpec((1, n_chunks, halo, cw),
                             lambda b, j: (b, 0, jnp.minimum((j + 1) * per_main, n_halo_blocks - 1), 0))
    tok_spec = pl.BlockSpec((1, main, d), lambda b, j: (b, j, 0))
    kernel = functools.partial(_na_layer_kernel, rows=rows)
    return pl.pallas_call(
        kernel,
        out_shape=jax.ShapeDtypeStruct(x.shape, F32),
        grid=(bsz, rows // ROWS_PER_STEP),
        in_specs=[main_spec, prev_spec, main_spec, next_spec, prev_spec, main_spec, next_spec,
                  tok_spec, tok_spec, _resident(tab.shape), _resident(w_out.shape),
                  _resident(ln_g.shape), _resident(ln_b.shape)],
        out_specs=tok_spec,
        scratch_shapes=[pltpu.VMEM((n_chunks, main + 2 * halo, cw), BF16),
                        pltpu.VMEM((n_chunks, main + 2 * halo, cw), BF16),
                        pltpu.VMEM((n_chunks, main, cw), F32)],
        compiler_params=pltpu.CompilerParams(
            dimension_semantics=("arbitrary", "arbitrary"),
            vmem_limit_bytes=60 * 1024 * 1024),
        name="na_layer",
    )(q, k, k, k, v, v, v, gate, x, tab, w_out, ln_g, ln_b)


def _trunk(x, p):
    x1 = _pool_layer(x, p["w_in_pool"], p["w_grp"], p["scale"], p["w_out_pool"], p["ln_g0"], p["ln_b0"])
    q, k, v, gate = _na_inproj(x1, p["w_in_na"])
    return _na_layer(q, k, v, gate, x1, p["tab"], p["w_out_na"], p["ln_g1"], p["ln_b1"])


def kernel(x_prompt, x_sample, w_in_pool, w_grp_pool, scale_pool, w_out_pool, w_in_na, rpb_na, w_out_na, ln_g, ln_b):
    p = {
        "w_in_pool": w_in_pool[0].astype(BF16),
        "w_grp": w_grp_pool[0].astype(BF16),
        "scale": scale_pool[0].reshape(1, -1).astype(F32),
        "w_out_pool": w_out_pool[0].astype(BF16),
        "w_in_na": w_in_na[0].astype(BF16),
        "w_out_na": w_out_na[0].astype(BF16),
        "tab": _bias_table(rpb_na[0]),
        "ln_g0": ln_g[0].reshape(1, -1).astype(F32),
        "ln_b0": ln_b[0].reshape(1, -1).astype(F32),
        "ln_g1": ln_g[1].reshape(1, -1).astype(F32),
        "ln_b1": ln_b[1].reshape(1, -1).astype(F32),
    }
    return (_trunk(x_prompt, p), _trunk(x_sample, p))
```

```python
import functools

import jax
import jax.numpy as jnp
import numpy as np
from jax import lax
from jax.experimental import pallas as pl
from jax.experimental.pallas import tpu as pltpu

F32 = jnp.float32
BF16 = jnp.bfloat16

GRID_W = 64
POOL_WINDOWS = (2, 4, 8, 16)
HEAD_DIM = 32
WIN_H = 8
WIN_W = 16
NA_SCALE = HEAD_DIM ** -0.5
LOG2_E = 1.4426950408889634
LN_EPS = 1e-5
DEPTH = 2
DEEPNORM_ALPHA = (2 * DEPTH) ** 0.25

V7X_SUBLANES_F32 = 8
V7X_LANES = 128
V7X_MXU_DIM = 256
V7X_VMEM_BYTES = 64 * 1024 * 1024

MASK_VALUE = -1e30
POOL_HALO = max(POOL_WINDOWS) // 2
HEADS_PER_CHUNK = V7X_MXU_DIM // HEAD_DIM
ROWS_PER_STEP = 8
HALO_ROWS = WIN_H // 2
V7X_SUBLANES_BF16 = 16
KEY_COLS = 2 * WIN_W
KEY_COL_ALIGN = V7X_SUBLANES_BF16
KEY_ROWS_PER_BLOCK = V7X_LANES // KEY_COLS


def _query_groups():
    groups = []
    for q in range(GRID_W):
        cs = min(max(q - WIN_W // 2, 0), GRID_W - WIN_W)
        k0 = min(cs // KEY_COL_ALIGN * KEY_COL_ALIGN, GRID_W - KEY_COLS)
        assert k0 <= cs and cs + WIN_W <= k0 + KEY_COLS
        if groups and groups[-1][2] == k0:
            groups[-1][1] = q + 1
        else:
            groups.append([q, q + 1, k0])
    assert all(g[0] % V7X_SUBLANES_F32 == 0 for g in groups)
    return tuple(tuple(g) for g in groups)


QUERY_GROUPS = _query_groups()
SCORE_LOOKAHEAD = 6
ROWS_PER_ITER = 8


def _silu(x):
    return x / (1.0 + jnp.exp(-x))


def _layer_norm(z, g, b):
    mu = jnp.mean(z, axis=-1, keepdims=True)
    zc = z - mu
    var = jnp.mean(zc * zc, axis=-1, keepdims=True)
    return zc * lax.rsqrt(var + LN_EPS) * g + b


def _fold_pool_weights_kernel(wu_ref, wgrp_ref, scale_ref, o_ref):
    folded = jnp.dot(wu_ref[...], wgrp_ref[0], preferred_element_type=F32, precision=lax.Precision.HIGHEST)
    o_ref[...] = (folded * scale_ref[...]).astype(o_ref.dtype)


def _fold_pool_weights(w_in, w_grp, scale):
    d = w_in.shape[0]
    n_groups, gw, _ = w_grp.shape
    return pl.pallas_call(
        _fold_pool_weights_kernel,
        out_shape=jax.ShapeDtypeStruct((d, n_groups * gw), BF16),
        grid=(n_groups,),
        in_specs=[pl.BlockSpec((d, gw), lambda g: (0, g)),
                  pl.BlockSpec((1, gw, gw), lambda g: (g, 0, 0)),
                  pl.BlockSpec((1, gw), lambda g: (0, g))],
        out_specs=pl.BlockSpec((d, gw), lambda g: (0, g)),
        compiler_params=pltpu.CompilerParams(dimension_semantics=("arbitrary",)),
        name="fold_pool_weights",
    )(w_in, w_grp, scale)


def _pool_layer_kernel(xp_ref, x_ref, xn_ref, wu_ref, wg_ref, wout_ref, g_ref, b_ref,
                       o_ref, xe_ref, xm_ref, *, seq, ts):
    j = pl.program_id(1)
    nj = pl.num_programs(1)
    x = x_ref[0]
    prev = jnp.where(j > 0, xp_ref[0], 0.0)
    nxt = jnp.where(j < nj - 1, xn_ref[0], 0.0)
    xe_ref[...] = jnp.concatenate([prev, x, nxt], axis=0).astype(BF16)
    xm_ref[...] = x.astype(BF16)

    n_ext = ts + 2 * POOL_HALO
    t = j * ts + lax.broadcasted_iota(jnp.int32, (ts, 1), 0)
    n_groups = len(POOL_WINDOWS)
    gw = wu_ref.shape[1] // n_groups

    def project(g):
        u = jnp.dot(xe_ref[...], wu_ref[:, g * gw:(g + 1) * gw], preferred_element_type=F32)
        gate = jnp.dot(xm_ref[...], wg_ref[:, g * gw:(g + 1) * gw], preferred_element_type=F32)
        return u, gate

    def mix(g, u, gate):
        h = POOL_WINDOWS[g] // 2
        b = u + pltpu.roll(u, 1, 0)
        span = 2
        while span < h:
            b = b + pltpu.roll(b, span, 0)
            span *= 2
        if h == 1:
            wsum = b
        else:
            wsum = pltpu.roll(b, n_ext - (h - 1), 0) + pltpu.roll(b, 1, 0)
        cnt = (jnp.minimum(t + h, seq) - jnp.maximum(t - h, 0)).astype(F32)
        y = wsum[POOL_HALO:POOL_HALO + ts] * (1.0 / cnt) - u[POOL_HALO:POOL_HALO + ts]
        return (y * _silu(gate)).astype(BF16)

    acc = jnp.zeros(o_ref.shape[1:], F32)
    ahead = project(0)
    for g in range(n_groups):
        cur = ahead
        if g + 1 < n_groups:
            ahead = project(g + 1)
        a = mix(g, *cur)
        acc = acc + jnp.dot(a, wout_ref[g * gw:(g + 1) * gw, :], preferred_element_type=F32)
    z = DEEPNORM_ALPHA * x + acc
    o_ref[0] = _layer_norm(z, g_ref[...], b_ref[...])


def _resident(shape):
    nd = len(shape)
    return pl.BlockSpec(shape, lambda *_: (0,) * nd, pipeline_mode=pl.Buffered(1))


def _pool_layer(x, w_u, w_gate, w_out, ln_g, ln_b, *, ts=512):
    bsz, seq, d = x.shape
    assert seq % ts == 0 and ts % POOL_HALO == 0
    n_halo_blocks = seq // POOL_HALO
    per_tile = ts // POOL_HALO
    kernel = functools.partial(_pool_layer_kernel, seq=seq, ts=ts)
    return pl.pallas_call(
        kernel,
        out_shape=jax.ShapeDtypeStruct(x.shape, F32),
        grid=(bsz, seq // ts),
        in_specs=[
            pl.BlockSpec((1, POOL_HALO, d), lambda b, j: (b, jnp.maximum(j * per_tile - 1, 0), 0)),
            pl.BlockSpec((1, ts, d), lambda b, j: (b, j, 0)),
            pl.BlockSpec((1, POOL_HALO, d), lambda b, j: (b, jnp.minimum((j + 1) * per_tile, n_halo_blocks - 1), 0)),
            _resident(w_u.shape), _resident(w_gate.shape), _resident(w_out.shape),
            _resident(ln_g.shape), _resident(ln_b.shape),
        ],
        out_specs=pl.BlockSpec((1, ts, d), lambda b, j: (b, j, 0)),
        scratch_shapes=[pltpu.VMEM((ts + 2 * POOL_HALO, d), BF16), pltpu.VMEM((ts, d), BF16)],
        compiler_params=pltpu.CompilerParams(
            dimension_semantics=("arbitrary", "arbitrary"),
            vmem_limit_bytes=48 * 1024 * 1024),
        name="pool_layer",
    )(x, x, x, w_u, w_gate, w_out, ln_g, ln_b)


def _na_inproj_kernel(x_ref, w_ref, q_ref, k_ref, v_ref, gate_ref):
    xb = x_ref[0].astype(BF16)
    width = gate_ref.shape[-1]
    n_chunks = q_ref.shape[1]
    cw = q_ref.shape[-1]
    for part, ref in enumerate((q_ref, k_ref, v_ref)):
        r = jnp.dot(xb, w_ref[:, part * width:(part + 1) * width], preferred_element_type=F32)
        if part == 0:
            r = r * (NA_SCALE * LOG2_E)
        r = r.astype(BF16)
        for c in range(n_chunks):
            ref[0, c] = r[:, c * cw:(c + 1) * cw]
    gate_ref[0] = jnp.dot(xb, w_ref[:, 3 * width:4 * width], preferred_element_type=F32)


def _na_inproj(x, w_in, *, tm=512):
    bsz, seq, d = x.shape
    width = w_in.shape[1] // 4
    n_chunks = width // V7X_MXU_DIM
    chunked = jax.ShapeDtypeStruct((bsz, n_chunks, seq, V7X_MXU_DIM), BF16)
    chunk_spec = pl.BlockSpec((1, n_chunks, tm, V7X_MXU_DIM), lambda b, j: (b, 0, j, 0))
    return pl.pallas_call(
        _na_inproj_kernel,
        out_shape=(chunked, chunked, chunked, jax.ShapeDtypeStruct((bsz, seq, width), F32)),
        grid=(bsz, seq // tm),
        in_specs=[pl.BlockSpec((1, tm, d), lambda b, j: (b, j, 0)), _resident(w_in.shape)],
        out_specs=(chunk_spec, chunk_spec, chunk_spec, pl.BlockSpec((1, tm, width), lambda b, j: (b, j, 0))),
        compiler_params=pltpu.CompilerParams(
            dimension_semantics=("arbitrary", "arbitrary"),
            vmem_limit_bytes=40 * 1024 * 1024),
        name="na_inproj",
    )(x, w_in)


def _kv_window_start(blk, rows):
    return jnp.clip(blk * ROWS_PER_STEP - HALO_ROWS, 0, rows - (ROWS_PER_STEP + 2 * HALO_ROWS))


def _na_layer_kernel(q_ref, kbuf, vbuf, gate_ref, x_ref, tab_ref, wout_ref, g_ref, b_ref, o_ref, obuf, *, rows):
    blk = pl.program_id(1)
    n_chunks = q_ref.shape[1]
    cw = q_ref.shape[-1]
    window_row0 = _kv_window_start(blk, rows)

    def head_masks_for(n_rows):
        lane_head = lax.broadcasted_iota(jnp.int32, (n_rows, cw), 1) // HEAD_DIM
        return [lane_head == h for h in range(HEADS_PER_CHUNK)]

    head_masks = {nq: head_masks_for(nq) for nq in sorted({q1 - q0 for q0, q1, _ in QUERY_GROUPS})}
    n_blocks = WIN_H // KEY_ROWS_PER_BLOCK

    def row_geometry(i):
        r = blk * ROWS_PER_STEP + i
        rs = jnp.clip(r - HALO_ROWS, 0, rows - WIN_H)
        return pl.multiple_of(i * GRID_W, GRID_W), (rs - window_row0) * GRID_W, r - rs

    def scores(geom, c, group):
        qoff, koff, row_off = geom
        q0, q1, k0 = group
        nq = q1 - q0
        qf = q_ref[0, c, pl.ds(qoff, GRID_W), :].astype(F32)
        starts = [pl.multiple_of(koff + kr * GRID_W + k0, KEY_COL_ALIGN) for kr in range(WIN_H)]
        kg = jnp.concatenate([kbuf[0, c, pl.ds(s0, KEY_COLS), :] for s0 in starts], axis=0)
        qs = jnp.concatenate([jnp.where(m, qf[q0:q1], 0.0) for m in head_masks[nq]],
                             axis=0).astype(BF16)
        s = lax.dot_general(qs, kg, (((1,), (1,)), ((), ())), preferred_element_type=F32)
        biased = []
        for h in range(HEADS_PER_CHUNK):
            bias = jnp.concatenate(
                [tab_ref[c * HEADS_PER_CHUNK + h, (WIN_H - 1) - row_off + KEY_ROWS_PER_BLOCK * m, q0:q1, :]
                 for m in range(n_blocks)], axis=1)
            biased.append(s[h * nq:(h + 1) * nq] + bias)
        s = jnp.concatenate(biased, axis=0)
        return (s, jnp.max(s, axis=-1, keepdims=True)), starts

    def finish(geom, c, group, s_and_max, starts):
        qoff = geom[0]
        q0, q1, _ = group
        nq = q1 - q0
        masks = head_masks[nq]
        s, s_max = s_and_max
        p = jnp.exp2(s - s_max)
        l = jnp.sum(p, axis=-1, keepdims=True)
        vg = jnp.concatenate([vbuf[0, c, pl.ds(s0, KEY_COLS), :] for s0 in starts], axis=0)
        pv = jnp.dot(p.astype(BF16), vg, preferred_element_type=F32)
        last = HEADS_PER_CHUNK - 1
        oc = pv[last * nq:]
        lc = jnp.broadcast_to(l[last * nq:], (nq, cw))
        for h in range(last - 1, -1, -1):
            oc = jnp.where(masks[h], pv[h * nq:(h + 1) * nq], oc)
            lc = jnp.where(masks[h], l[h * nq:(h + 1) * nq], lc)
        obuf[c, pl.ds(pl.multiple_of(qoff + q0, V7X_SUBLANES_F32), nq), :] = oc * (1.0 / lc)

    def rows_body(it, carry):
        units = [(row_geometry(it * ROWS_PER_ITER + sub), c, g)
                 for sub in range(ROWS_PER_ITER) for c in range(n_chunks) for g in QUERY_GROUPS]
        pending = []
        for unit in units:
            pending.append((unit, scores(*unit)))
            if len(pending) > SCORE_LOOKAHEAD:
                done, (s, starts) = pending.pop(0)
                finish(*done, s, starts)
        for done, (s, starts) in pending:
            finish(*done, s, starts)
        return carry

    lax.fori_loop(0, ROWS_PER_STEP // ROWS_PER_ITER, rows_body, 0)

    o_all = jnp.concatenate([obuf[c] for c in range(n_chunks)], axis=1)
    a = (o_all * _silu(gate_ref[0])).astype(BF16)
    h2 = jnp.dot(a, wout_ref[...], preferred_element_type=F32)
    z = DEEPNORM_ALPHA * x_ref[0] + h2
    o_ref[0] = _layer_norm(z, g_ref[...], b_ref[...])


def _bias_table_kernel(y_ref, o_ref):
    q = lax.broadcasted_iota(jnp.int32, (GRID_W, V7X_LANES), 0)
    kcc = lax.broadcasted_iota(jnp.int32, (GRID_W, V7X_LANES), 1) % KEY_COLS
    cs = jnp.clip(q - WIN_W // 2, 0, GRID_W - WIN_W)
    k0 = jnp.zeros_like(q)
    for q0, _, first_key in QUERY_GROUPS:
        k0 = jnp.where(q >= q0, first_key, k0)
    kc = k0 + kcc
    valid = (kc >= cs) & (kc < cs + WIN_W)
    for d in range(o_ref.shape[1]):
        yb = jnp.broadcast_to(y_ref[0, d:d + 1, :], (GRID_W, V7X_LANES))
        t = jnp.zeros((GRID_W, V7X_LANES), F32)
        for q0, _, first_key in QUERY_GROUPS:
            shift = (-(first_key + WIN_W - 1)) % V7X_LANES
            rolled = pltpu.roll(yb, shift, 1, stride=1, stride_axis=0)
            t = jnp.where(q >= q0, rolled, t)
        o_ref[0, d] = jnp.where(valid, t * LOG2_E, MASK_VALUE)


def _bias_table(rpb):
    n_heads = rpb.shape[0]
    n_d = 2 * WIN_H - KEY_ROWS_PER_BLOCK
    padded = jnp.pad(rpb.astype(F32), ((0, 0), (0, 0), (0, KEY_COLS - rpb.shape[2])))
    y = jnp.stack([padded[:, d:d + KEY_ROWS_PER_BLOCK].reshape(n_heads, V7X_LANES) for d in range(n_d)], axis=1)
    return pl.pallas_call(
        _bias_table_kernel,
        out_shape=jax.ShapeDtypeStruct((n_heads, n_d, GRID_W, V7X_LANES), F32),
        grid=(n_heads,),
        in_specs=[pl.BlockSpec((1, n_d, V7X_LANES), lambda h: (h, 0, 0))],
        out_specs=pl.BlockSpec((1, n_d, GRID_W, V7X_LANES), lambda h: (h, 0, 0, 0)),
        compiler_params=pltpu.CompilerParams(dimension_semantics=("arbitrary",)),
        name="bias_table",
    )(y)


def _na_layer(q, k, v, gate, x, tab, w_out, ln_g, ln_b):
    bsz, seq, d = x.shape
    rows = seq // GRID_W
    n_chunks, cw = q.shape[1], q.shape[3]
    main = ROWS_PER_STEP * GRID_W
    window = (ROWS_PER_STEP + 2 * HALO_ROWS) * GRID_W
    assert rows % ROWS_PER_STEP == 0 and rows * GRID_W >= window
    main_spec = pl.BlockSpec((1, n_chunks, main, cw), lambda b, j: (b, 0, j, 0))
    kv_spec = pl.BlockSpec((pl.Element(1), pl.Element(n_chunks), pl.Element(window), pl.Element(cw)),
                           lambda b, j: (b, 0, _kv_window_start(j, rows) * GRID_W, 0))
    tok_spec = pl.BlockSpec((1, main, d), lambda b, j: (b, j, 0))
    kernel = functools.partial(_na_layer_kernel, rows=rows)
    return pl.pallas_call(
        kernel,
        out_shape=jax.ShapeDtypeStruct(x.shape, F32),
        grid=(bsz, rows // ROWS_PER_STEP),
        in_specs=[main_spec, kv_spec, kv_spec, tok_spec, tok_spec, _resident(tab.shape), _resident(w_out.shape),
                  _resident(ln_g.shape), _resident(ln_b.shape)],
        out_specs=tok_spec,
        scratch_shapes=[pltpu.VMEM((n_chunks, main, cw), F32)],
        compiler_params=pltpu.CompilerParams(
            dimension_semantics=("arbitrary", "arbitrary"),
            vmem_limit_bytes=60 * 1024 * 1024),
        name="na_layer",
    )(q, k, v, gate, x, tab, w_out, ln_g, ln_b)


def _trunk(x, p):
    x1 = _pool_layer(x, p["w_u_pool"], p["w_gate_pool"], p["w_out_pool"], p["ln_g0"], p["ln_b0"])
    q, k, v, gate = _na_inproj(x1, p["w_in_na"])
    return _na_layer(q, k, v, gate, x1, p["tab"], p["w_out_na"], p["ln_g1"], p["ln_b1"])


def kernel(x_prompt, x_sample, w_in_pool, w_grp_pool, scale_pool, w_out_pool, w_in_na, rpb_na, w_out_na, ln_g, ln_b):
    pool_width = w_out_pool.shape[1]
    scale = scale_pool[0].reshape(1, -1).astype(F32)
    p = {
        "w_u_pool": _fold_pool_weights(w_in_pool[0], w_grp_pool[0], scale),
        "w_gate_pool": w_in_pool[0, :, pool_width:].astype(BF16),
        "w_out_pool": w_out_pool[0].astype(BF16),
        "w_in_na": w_in_na[0].astype(BF16),
        "w_out_na": w_out_na[0].astype(BF16),
        "tab": _bias_table(rpb_na[0]),
        "ln_g0": ln_g[0].reshape(1, -1).astype(F32),
        "ln_b0": ln_b[0].reshape(1, -1).astype(F32),
        "ln_g1": ln_g[1].reshape(1, -1).astype(F32),
        "ln_b1": ln_b[1].reshape(1, -1).astype(F32),
    }
    return (_trunk(x_prompt, p), _trunk(x_sample, p))
```

```python
import functools

import jax
import jax.numpy as jnp
import numpy as np
from jax import lax
from jax.experimental import pallas as pl
from jax.experimental.pallas import tpu as pltpu

F32 = jnp.float32
BF16 = jnp.bfloat16

GRID_W = 64
POOL_WINDOWS = (2, 4, 8, 16)
HEAD_DIM = 32
WIN_H = 8
WIN_W = 16
NA_SCALE = HEAD_DIM ** -0.5
LOG2_E = 1.4426950408889634
LN_EPS = 1e-5
DEPTH = 2
DEEPNORM_ALPHA = (2 * DEPTH) ** 0.25

V7X_SUBLANES_F32 = 8
V7X_LANES = 128
V7X_MXU_DIM = 256
V7X_VMEM_BYTES = 64 * 1024 * 1024

MASK_VALUE = -1e30
POOL_HALO = max(POOL_WINDOWS) // 2
HEADS_PER_CHUNK = V7X_MXU_DIM // HEAD_DIM
ROWS_PER_STEP = 8
HALO_ROWS = WIN_H // 2
V7X_SUBLANES_BF16 = 16
KEY_COLS = 2 * WIN_W
KEY_COL_ALIGN = V7X_SUBLANES_BF16
KEY_ROWS_PER_BLOCK = V7X_LANES // KEY_COLS


def _query_groups():
    groups = []
    for q in range(GRID_W):
        cs = min(max(q - WIN_W // 2, 0), GRID_W - WIN_W)
        k0 = min(cs // KEY_COL_ALIGN * KEY_COL_ALIGN, GRID_W - KEY_COLS)
        assert k0 <= cs and cs + WIN_W <= k0 + KEY_COLS
        if groups and groups[-1][2] == k0:
            groups[-1][1] = q + 1
        else:
            groups.append([q, q + 1, k0])
    assert all(g[0] % V7X_SUBLANES_F32 == 0 for g in groups)
    return tuple(tuple(g) for g in groups)


QUERY_GROUPS = _query_groups()
SCORE_LOOKAHEAD = 6
POOL_SUBTILES = 1
EPILOGUE_ROW_SPLITS = (6, 2)


def _silu(x):
    return x / (1.0 + jnp.exp(-x))


def _layer_norm(z, g, b):
    mu = jnp.mean(z, axis=-1, keepdims=True)
    zc = z - mu
    var = jnp.mean(zc * zc, axis=-1, keepdims=True)
    return zc * lax.rsqrt(var + LN_EPS) * g + b


def _fold_pool_weights_kernel(wu_ref, wgate_ref, wgrp_ref, scale_ref, o_ref):
    gw = wu_ref.shape[1]
    folded = jnp.dot(wu_ref[...], wgrp_ref[0], preferred_element_type=F32, precision=lax.Precision.HIGHEST)
    o_ref[:, :gw] = (folded * scale_ref[...]).astype(o_ref.dtype)
    o_ref[:, gw:] = wgate_ref[...].astype(o_ref.dtype)


def _fold_pool_weights(w_in, w_grp, scale):
    d = w_in.shape[0]
    n_groups, gw, _ = w_grp.shape
    return pl.pallas_call(
        _fold_pool_weights_kernel,
        out_shape=jax.ShapeDtypeStruct((d, 2 * n_groups * gw), BF16),
        grid=(n_groups,),
        in_specs=[pl.BlockSpec((d, gw), lambda g: (0, g)),
                  pl.BlockSpec((d, gw), lambda g: (0, n_groups + g)),
                  pl.BlockSpec((1, gw, gw), lambda g: (g, 0, 0)),
                  pl.BlockSpec((1, gw), lambda g: (0, g))],
        out_specs=pl.BlockSpec((d, 2 * gw), lambda g: (0, g)),
        compiler_params=pltpu.CompilerParams(dimension_semantics=("arbitrary",)),
        name="fold_pool_weights",
    )(w_in, w_in, w_grp, scale)


def _pool_layer_kernel(xp_ref, x_ref, xn_ref, win_ref, wout_ref, g_ref, b_ref, o_ref, xe_ref, *, seq, ts):
    j = pl.program_id(1)
    nj = pl.num_programs(1)
    x = x_ref[0]
    prev = jnp.where(j > 0, xp_ref[0], 0.0)
    nxt = jnp.where(j < nj - 1, xn_ref[0], 0.0)
    xe_ref[...] = jnp.concatenate([prev, x, nxt], axis=0).astype(BF16)

    tsub = ts // POOL_SUBTILES
    n_ext = tsub + 2 * POOL_HALO
    n_groups = len(POOL_WINDOWS)
    gw = win_ref.shape[1] // (2 * n_groups)

    def project(sub, g):
        r0 = sub * tsub
        r = jnp.dot(xe_ref[r0:r0 + n_ext], win_ref[:, 2 * g * gw:2 * (g + 1) * gw],
                    preferred_element_type=F32)
        return r[:, :gw], r[POOL_HALO:POOL_HALO + tsub, gw:]

    def mix(sub, g, u, gate):
        h = POOL_WINDOWS[g] // 2
        b = u + pltpu.roll(u, 1, 0)
        span = 2
        while span < h:
            b = b + pltpu.roll(b, span, 0)
            span *= 2
        if h == 1:
            wsum = b
        else:
            wsum = pltpu.roll(b, n_ext - (h - 1), 0) + pltpu.roll(b, 1, 0)
        t = j * ts + sub * tsub + lax.broadcasted_iota(jnp.int32, (tsub, 1), 0)
        cnt = (jnp.minimum(t + h, seq) - jnp.maximum(t - h, 0)).astype(F32)
        y = wsum[POOL_HALO:POOL_HALO + tsub] * (1.0 / cnt) - u[POOL_HALO:POOL_HALO + tsub]
        return (y * _silu(gate)).astype(BF16)

    units = [(sub, g) for sub in range(POOL_SUBTILES) for g in range(n_groups)]
    ahead = project(*units[0])
    acc = None
    for n, (sub, g) in enumerate(units):
        cur = ahead
        if n + 1 < len(units):
            ahead = project(*units[n + 1])
        a = mix(sub, g, *cur)
        part = jnp.dot(a, wout_ref[g * gw:(g + 1) * gw, :], preferred_element_type=F32)
        acc = part if g == 0 else acc + part
        if g == n_groups - 1:
            rows_ = slice(sub * tsub, (sub + 1) * tsub)
            z = DEEPNORM_ALPHA * x_ref[0, rows_, :] + acc
            o_ref[0, rows_, :] = _layer_norm(z, g_ref[...], b_ref[...])


def _resident(shape):
    nd = len(shape)
    return pl.BlockSpec(shape, lambda *_: (0,) * nd, pipeline_mode=pl.Buffered(1))


def _pool_layer(x, w_in, w_out, ln_g, ln_b, *, ts=1024):
    bsz, seq, d = x.shape
    assert seq % ts == 0 and ts % POOL_HALO == 0
    n_halo_blocks = seq // POOL_HALO
    per_tile = ts // POOL_HALO
    kernel = functools.partial(_pool_layer_kernel, seq=seq, ts=ts)
    return pl.pallas_call(
        kernel,
        out_shape=jax.ShapeDtypeStruct(x.shape, F32),
        grid=(bsz, seq // ts),
        in_specs=[
            pl.BlockSpec((1, POOL_HALO, d), lambda b, j: (b, jnp.maximum(j * per_tile - 1, 0), 0)),
            pl.BlockSpec((1, ts, d), lambda b, j: (b, j, 0)),
            pl.BlockSpec((1, POOL_HALO, d), lambda b, j: (b, jnp.minimum((j + 1) * per_tile, n_halo_blocks - 1), 0)),
            _resident(w_in.shape), _resident(w_out.shape), _resident(ln_g.shape), _resident(ln_b.shape),
        ],
        out_specs=pl.BlockSpec((1, ts, d), lambda b, j: (b, j, 0)),
        scratch_shapes=[pltpu.VMEM((ts + 2 * POOL_HALO, d), BF16)],
        compiler_params=pltpu.CompilerParams(
            dimension_semantics=("arbitrary", "arbitrary"),
            vmem_limit_bytes=60 * 1024 * 1024),
        name="pool_layer",
    )(x, x, x, w_in, w_out, ln_g, ln_b)


def _na_inproj_kernel(x_ref, w_ref, q_ref, k_ref, v_ref, gate_ref):
    xb = x_ref[0].astype(BF16)
    width = gate_ref.shape[-1]
    n_chunks = q_ref.shape[1]
    cw = q_ref.shape[-1]
    for part, ref in enumerate((q_ref, k_ref, v_ref)):
        r = jnp.dot(xb, w_ref[:, part * width:(part + 1) * width], preferred_element_type=F32)
        if part == 0:
            r = r * (NA_SCALE * LOG2_E)
        r = r.astype(BF16)
        for c in range(n_chunks):
            ref[0, c] = r[:, c * cw:(c + 1) * cw]
    gate_ref[0] = _silu(jnp.dot(xb, w_ref[:, 3 * width:4 * width], preferred_element_type=F32))


def _na_inproj(x, w_in, *, tm=1024):
    bsz, seq, d = x.shape
    width = w_in.shape[1] // 4
    n_chunks = width // V7X_MXU_DIM
    chunked = jax.ShapeDtypeStruct((bsz, n_chunks, seq, V7X_MXU_DIM), BF16)
    chunk_spec = pl.BlockSpec((1, n_chunks, tm, V7X_MXU_DIM), lambda b, j: (b, 0, j, 0))
    return pl.pallas_call(
        _na_inproj_kernel,
        out_shape=(chunked, chunked, chunked, jax.ShapeDtypeStruct((bsz, seq, width), F32)),
        grid=(bsz, seq // tm),
        in_specs=[pl.BlockSpec((1, tm, d), lambda b, j: (b, j, 0)), _resident(w_in.shape)],
        out_specs=(chunk_spec, chunk_spec, chunk_spec, pl.BlockSpec((1, tm, width), lambda b, j: (b, j, 0))),
        compiler_params=pltpu.CompilerParams(
            dimension_semantics=("arbitrary", "arbitrary"),
            vmem_limit_bytes=56 * 1024 * 1024),
        name="na_inproj",
    )(x, w_in)


def _kv_window_start(blk, rows):
    return jnp.clip(blk * ROWS_PER_STEP - HALO_ROWS, 0, rows - (ROWS_PER_STEP + 2 * HALO_ROWS))


def _na_layer_kernel(q_ref, kbuf, vbuf, sgate_ref, x_ref, tab_ref, wout_ref, g_ref, b_ref, o_ref, obuf, *, rows):
    blk = pl.program_id(1)
    n_chunks = q_ref.shape[1]
    cw = q_ref.shape[-1]
    window_row0 = _kv_window_start(blk, rows)

    def head_masks_for(n_rows):
        lane_head = lax.broadcasted_iota(jnp.int32, (n_rows, cw), 1) // HEAD_DIM
        return [lane_head == h for h in range(HEADS_PER_CHUNK)]

    head_masks = {nq: head_masks_for(nq) for nq in sorted({q1 - q0 for q0, q1, _ in QUERY_GROUPS})}
    n_blocks = WIN_H // KEY_ROWS_PER_BLOCK

    def row_geometry(i):
        r = blk * ROWS_PER_STEP + i
        rs = jnp.clip(r - HALO_ROWS, 0, rows - WIN_H)
        return i * GRID_W, (rs - window_row0) * GRID_W, r - rs

    def scores(geom, c, group):
        qoff, koff, row_off = geom
        q0, q1, k0 = group
        nq = q1 - q0
        qf = q_ref[0, c, qoff:qoff + GRID_W, :].astype(F32)
        starts = [pl.multiple_of(koff + kr * GRID_W + k0, KEY_COL_ALIGN) for kr in range(WIN_H)]
        kg = jnp.concatenate([kbuf[0, c, pl.ds(s0, KEY_COLS), :] for s0 in starts], axis=0)
        qs = jnp.concatenate([jnp.where(m, qf[q0:q1], 0.0) for m in head_masks[nq]],
                             axis=0).astype(BF16)
        s = lax.dot_general(qs, kg, (((1,), (1,)), ((), ())), preferred_element_type=F32)
        biased = []
        for h in range(HEADS_PER_CHUNK):
            bias = jnp.concatenate(
                [tab_ref[c * HEADS_PER_CHUNK + h, (WIN_H - 1) - row_off + KEY_ROWS_PER_BLOCK * m, q0:q1, :]
                 for m in range(n_blocks)], axis=1)
            biased.append(s[h * nq:(h + 1) * nq] + bias)
        s = jnp.concatenate(biased, axis=0)
        return (s, jnp.max(s, axis=-1, keepdims=True)), starts

    def finish(geom, c, group, s_and_max, starts):
        qoff = geom[0]
        q0, q1, _ = group
        nq = q1 - q0
        masks = head_masks[nq]
        s, s_max = s_and_max
        p = jnp.exp2(s - s_max)
        l = jnp.sum(p, axis=-1, keepdims=True)
        vg = jnp.concatenate([vbuf[0, c, pl.ds(s0, KEY_COLS), :] for s0 in starts], axis=0)
        pv = jnp.dot(p.astype(BF16), vg, preferred_element_type=F32)
        last = HEADS_PER_CHUNK - 1
        oc = pv[last * nq:]
        lc = jnp.broadcast_to(l[last * nq:], (nq, cw))
        for h in range(last - 1, -1, -1):
            oc = jnp.where(masks[h], pv[h * nq:(h + 1) * nq], oc)
            lc = jnp.where(masks[h], l[h * nq:(h + 1) * nq], lc)
        obuf[c, qoff + q0:qoff + q1, :] = oc * (1.0 / lc)

    def epilogue_pieces(row0, row1):
        tok = slice(row0 * GRID_W, row1 * GRID_W)
        state = {}

        def gate():
            o_part = jnp.concatenate([obuf[c, tok, :] for c in range(n_chunks)], axis=1)
            state["a"] = (o_part * sgate_ref[0, tok, :]).astype(BF16)

        def project(nb):
            state[nb] = jnp.dot(state["a"], wout_ref[:, nb * cw:(nb + 1) * cw], preferred_element_type=F32)

        def normalise():
            h2 = jnp.concatenate([state[nb] for nb in range(n_chunks)], axis=1)
            z = DEEPNORM_ALPHA * x_ref[0, tok, :] + h2
            o_ref[0, tok, :] = _layer_norm(z, g_ref[...], b_ref[...])

        return [gate] + [functools.partial(project, nb) for nb in range(n_chunks)] + [normalise]

    units = [(i, c, g) for i in range(ROWS_PER_STEP) for c in range(n_chunks) for g in QUERY_GROUPS]
    geoms = [row_geometry(i) for i in range(ROWS_PER_STEP)]
    units_per_row = n_chunks * len(QUERY_GROUPS)
    part_ends = np.cumsum(EPILOGUE_ROW_SPLITS)
    assert part_ends[-1] == ROWS_PER_STEP
    pending, epilogue = [], []
    n_finished = 0
    for n_scored in range(len(units) + SCORE_LOOKAHEAD):
        if n_scored < len(units):
            i, c, g = units[n_scored]
            pending.append(((geoms[i], c, g), scores(geoms[i], c, g)))
        if n_scored >= SCORE_LOOKAHEAD:
            done, (s, starts) = pending.pop(0)
            finish(*done, s, starts)
            n_finished += 1
            if epilogue:
                epilogue.pop(0)()
            if n_finished % units_per_row == 0 and n_finished // units_per_row in part_ends:
                row1 = n_finished // units_per_row
                row0 = row1 - EPILOGUE_ROW_SPLITS[list(part_ends).index(row1)]
                epilogue += epilogue_pieces(row0, row1)
    for piece in epilogue:
        piece()


def _bias_table_kernel(y_ref, o_ref):
    q = lax.broadcasted_iota(jnp.int32, (GRID_W, V7X_LANES), 0)
    kcc = lax.broadcasted_iota(jnp.int32, (GRID_W, V7X_LANES), 1) % KEY_COLS
    cs = jnp.clip(q - WIN_W // 2, 0, GRID_W - WIN_W)
    k0 = jnp.zeros_like(q)
    for q0, _, first_key in QUERY_GROUPS:
        k0 = jnp.where(q >= q0, first_key, k0)
    kc = k0 + kcc
    valid = (kc >= cs) & (kc < cs + WIN_W)
    for d in range(o_ref.shape[1]):
        yb = jnp.broadcast_to(y_ref[0, d:d + 1, :], (GRID_W, V7X_LANES))
        t = jnp.zeros((GRID_W, V7X_LANES), F32)
        for q0, _, first_key in QUERY_GROUPS:
            shift = (-(first_key + WIN_W - 1)) % V7X_LANES
            rolled = pltpu.roll(yb, shift, 1, stride=1, stride_axis=0)
            t = jnp.where(q >= q0, rolled, t)
        o_ref[0, d] = jnp.where(valid, t * LOG2_E, MASK_VALUE)


def _bias_table(rpb):
    n_heads = rpb.shape[0]
    n_d = 2 * WIN_H - KEY_ROWS_PER_BLOCK
    padded = jnp.pad(rpb.astype(F32), ((0, 0), (0, 0), (0, KEY_COLS - rpb.shape[2])))
    y = jnp.stack([padded[:, d:d + KEY_ROWS_PER_BLOCK].reshape(n_heads, V7X_LANES) for d in range(n_d)], axis=1)
    return pl.pallas_call(
        _bias_table_kernel,
        out_shape=jax.ShapeDtypeStruct((n_heads, n_d, GRID_W, V7X_LANES), F32),
        grid=(n_heads,),
        in_specs=[pl.BlockSpec((1, n_d, V7X_LANES), lambda h: (h, 0, 0))],
        out_specs=pl.BlockSpec((1, n_d, GRID_W, V7X_LANES), lambda h: (h, 0, 0, 0)),
        compiler_params=pltpu.CompilerParams(dimension_semantics=("arbitrary",)),
        name="bias_table",
    )(y)


def _na_layer(q, k, v, gate, x, tab, w_out, ln_g, ln_b):
    bsz, seq, d = x.shape
    rows = seq // GRID_W
    n_chunks, cw = q.shape[1], q.shape[3]
    main = ROWS_PER_STEP * GRID_W
    window = (ROWS_PER_STEP + 2 * HALO_ROWS) * GRID_W
    assert rows % ROWS_PER_STEP == 0 and rows * GRID_W >= window
    main_spec = pl.BlockSpec((1, n_chunks, main, cw), lambda b, j: (b, 0, j, 0))
    kv_spec = pl.BlockSpec((pl.Element(1), pl.Element(n_chunks), pl.Element(window), pl.Element(cw)),
                           lambda b, j: (b, 0, _kv_window_start(j, rows) * GRID_W, 0))
    tok_spec = pl.BlockSpec((1, main, d), lambda b, j: (b, j, 0))
    kernel = functools.partial(_na_layer_kernel, rows=rows)
    return pl.pallas_call(
        kernel,
        out_shape=jax.ShapeDtypeStruct(x.shape, F32),
        grid=(bsz, rows // ROWS_PER_STEP),
        in_specs=[main_spec, kv_spec, kv_spec, tok_spec, tok_spec, _resident(tab.shape), _resident(w_out.shape),
                  _resident(ln_g.shape), _resident(ln_b.shape)],
        out_specs=tok_spec,
        scratch_shapes=[pltpu.VMEM((n_chunks, main, cw), F32)],
        compiler_params=pltpu.CompilerParams(
            dimension_semantics=("arbitrary", "arbitrary"),
            vmem_limit_bytes=60 * 1024 * 1024),
        name="na_layer",
    )(q, k, v, gate, x, tab, w_out, ln_g, ln_b)


def _trunk(x, p):
    x1 = _pool_layer(x, p["w_in_pool"], p["w_out_pool"], p["ln_g0"], p["ln_b0"])
    q, k, v, sgate = _na_inproj(x1, p["w_in_na"])
    return _na_layer(q, k, v, sgate, x1, p["tab"], p["w_out_na"], p["ln_g1"], p["ln_b1"])


def kernel(x_prompt, x_sample, w_in_pool, w_grp_pool, scale_pool, w_out_pool, w_in_na, rpb_na, w_out_na, ln_g, ln_b):
    scale = scale_pool[0].reshape(1, -1).astype(F32)
    p = {
        "w_in_pool": _fold_pool_weights(w_in_pool[0], w_grp_pool[0], scale),
        "w_out_pool": w_out_pool[0].astype(BF16),
        "w_in_na": w_in_na[0].astype(BF16),
        "w_out_na": w_out_na[0].astype(BF16),
        "tab": _bias_table(rpb_na[0]),
        "ln_g0": ln_g[0].reshape(1, -1).astype(F32),
        "ln_b0": ln_b[0].reshape(1, -1).astype(F32),
        "ln_g1": ln_g[1].reshape(1, -1).astype(F32),
        "ln_b1": ln_b[1].reshape(1, -1).astype(F32),
    }
    return (_trunk(x_prompt, p), _trunk(x_sample, p))
```

```python
import functools

import jax
import jax.numpy as jnp
import numpy as np
from jax import lax
from jax.experimental import pallas as pl
from jax.experimental.pallas import tpu as pltpu

F32 = jnp.float32
BF16 = jnp.bfloat16

GRID_W = 64
POOL_WINDOWS = (2, 4, 8, 16)
HEAD_DIM = 32
WIN_H = 8
WIN_W = 16
NA_SCALE = HEAD_DIM ** -0.5
LOG2_E = 1.4426950408889634
LN_EPS = 1e-5
DEPTH = 2
DEEPNORM_ALPHA = (2 * DEPTH) ** 0.25

V7X_SUBLANES_F32 = 8
V7X_LANES = 128
V7X_MXU_DIM = 256
V7X_VMEM_BYTES = 64 * 1024 * 1024

MASK_VALUE = -1e30
POOL_HALO = max(POOL_WINDOWS) // 2
HEADS_PER_CHUNK = V7X_MXU_DIM // HEAD_DIM
ROWS_PER_STEP = 8
HALO_ROWS = WIN_H // 2
V7X_SUBLANES_BF16 = 16
KEY_COLS = 2 * WIN_W
KEY_COL_ALIGN = V7X_SUBLANES_BF16
KEY_ROWS_PER_BLOCK = V7X_LANES // KEY_COLS


def _query_groups():
    groups = []
    for q in range(GRID_W):
        cs = min(max(q - WIN_W // 2, 0), GRID_W - WIN_W)
        k0 = min(cs // KEY_COL_ALIGN * KEY_COL_ALIGN, GRID_W - KEY_COLS)
        assert k0 <= cs and cs + WIN_W <= k0 + KEY_COLS
        if groups and groups[-1][2] == k0:
            groups[-1][1] = q + 1
        else:
            groups.append([q, q + 1, k0])
    assert all(g[0] % V7X_SUBLANES_F32 == 0 for g in groups)
    return tuple(tuple(g) for g in groups)


QUERY_GROUPS = _query_groups()
SCORE_LOOKAHEAD = 4
POOL_SUBTILES = 1
EPILOGUE_ROW_SPLITS = (6, 2)


def _silu(x):
    return x / (1.0 + jnp.exp(-x))


def _layer_norm(z, g, b):
    mu = jnp.mean(z, axis=-1, keepdims=True)
    zc = z - mu
    var = jnp.mean(zc * zc, axis=-1, keepdims=True)
    return zc * lax.rsqrt(var + LN_EPS) * g + b


def _fold_pool_weights_kernel(wu_ref, wgate_ref, wgrp_ref, scale_ref, o_ref):
    gw = wu_ref.shape[1]
    folded = jnp.dot(wu_ref[...], wgrp_ref[0], preferred_element_type=F32, precision=lax.Precision.HIGHEST)
    o_ref[:, :gw] = (folded * scale_ref[...]).astype(o_ref.dtype)
    o_ref[:, gw:] = wgate_ref[...].astype(o_ref.dtype)


def _fold_pool_weights(w_in, w_grp, scale):
    d = w_in.shape[0]
    n_groups, gw, _ = w_grp.shape
    return pl.pallas_call(
        _fold_pool_weights_kernel,
        out_shape=jax.ShapeDtypeStruct((d, 2 * n_groups * gw), BF16),
        grid=(n_groups,),
        in_specs=[pl.BlockSpec((d, gw), lambda g: (0, g)),
                  pl.BlockSpec((d, gw), lambda g: (0, n_groups + g)),
                  pl.BlockSpec((1, gw, gw), lambda g: (g, 0, 0)),
                  pl.BlockSpec((1, gw), lambda g: (0, g))],
        out_specs=pl.BlockSpec((d, 2 * gw), lambda g: (0, g)),
        compiler_params=pltpu.CompilerParams(dimension_semantics=("arbitrary",)),
        name="fold_pool_weights",
    )(w_in, w_in, w_grp, scale)


def _pool_layer_kernel(xp_ref, x_ref, xn_ref, win_ref, wout_ref, g_ref, b_ref, o_ref, xe_ref, *, seq, ts):
    j = pl.program_id(1)
    nj = pl.num_programs(1)
    x = x_ref[0]
    prev = jnp.where(j > 0, xp_ref[0], 0.0)
    nxt = jnp.where(j < nj - 1, xn_ref[0], 0.0)
    xe_ref[...] = jnp.concatenate([prev, x, nxt], axis=0).astype(BF16)

    tsub = ts // POOL_SUBTILES
    n_ext = tsub + 2 * POOL_HALO
    n_groups = len(POOL_WINDOWS)
    gw = win_ref.shape[1] // (2 * n_groups)

    def project(sub, g):
        r0 = sub * tsub
        r = jnp.dot(xe_ref[r0:r0 + n_ext], win_ref[:, 2 * g * gw:2 * (g + 1) * gw],
                    preferred_element_type=F32)
        return r[:, :gw], r[POOL_HALO:POOL_HALO + tsub, gw:]

    def mix(sub, g, u, gate):
        h = POOL_WINDOWS[g] // 2
        b = u + pltpu.roll(u, 1, 0)
        span = 2
        while span < h:
            b = b + pltpu.roll(b, span, 0)
            span *= 2
        if h == 1:
            wsum = b
        else:
            wsum = pltpu.roll(b, n_ext - (h - 1), 0) + pltpu.roll(b, 1, 0)
        t = j * ts + sub * tsub + lax.broadcasted_iota(jnp.int32, (tsub, 1), 0)
        cnt = (jnp.minimum(t + h, seq) - jnp.maximum(t - h, 0)).astype(F32)
        y = wsum[POOL_HALO:POOL_HALO + tsub] * (1.0 / cnt) - u[POOL_HALO:POOL_HALO + tsub]
        return (y * _silu(gate)).astype(BF16)

    units = [(sub, g) for sub in range(POOL_SUBTILES) for g in range(n_groups)]
    ahead = project(*units[0])
    acc = None
    for n, (sub, g) in enumerate(units):
        cur = ahead
        if n + 1 < len(units):
            ahead = project(*units[n + 1])
        a = mix(sub, g, *cur)
        part = jnp.dot(a, wout_ref[g * gw:(g + 1) * gw, :], preferred_element_type=F32)
        acc = part if g == 0 else acc + part
        if g == n_groups - 1:
            rows_ = slice(sub * tsub, (sub + 1) * tsub)
            z = DEEPNORM_ALPHA * x_ref[0, rows_, :] + acc
            o_ref[0, rows_, :] = _layer_norm(z, g_ref[...], b_ref[...])


def _resident(shape):
    nd = len(shape)
    return pl.BlockSpec(shape, lambda *_: (0,) * nd, pipeline_mode=pl.Buffered(1))


def _pool_layer(x, w_in, w_out, ln_g, ln_b, *, ts=1024):
    bsz, seq, d = x.shape
    assert seq % ts == 0 and ts % POOL_HALO == 0
    n_halo_blocks = seq // POOL_HALO
    per_tile = ts // POOL_HALO
    kernel = functools.partial(_pool_layer_kernel, seq=seq, ts=ts)
    return pl.pallas_call(
        kernel,
        out_shape=jax.ShapeDtypeStruct(x.shape, F32),
        grid=(bsz, seq // ts),
        in_specs=[
            pl.BlockSpec((1, POOL_HALO, d), lambda b, j: (b, jnp.maximum(j * per_tile - 1, 0), 0)),
            pl.BlockSpec((1, ts, d), lambda b, j: (b, j, 0)),
            pl.BlockSpec((1, POOL_HALO, d), lambda b, j: (b, jnp.minimum((j + 1) * per_tile, n_halo_blocks - 1), 0)),
            _resident(w_in.shape), _resident(w_out.shape), _resident(ln_g.shape), _resident(ln_b.shape),
        ],
        out_specs=pl.BlockSpec((1, ts, d), lambda b, j: (b, j, 0)),
        scratch_shapes=[pltpu.VMEM((ts + 2 * POOL_HALO, d), BF16)],
        compiler_params=pltpu.CompilerParams(
            dimension_semantics=("arbitrary", "arbitrary"),
            vmem_limit_bytes=60 * 1024 * 1024),
        name="pool_layer",
    )(x, x, x, w_in, w_out, ln_g, ln_b)


def _na_inproj_kernel(x_ref, w_ref, q_ref, k_ref, v_ref, gate_ref):
    xb = x_ref[0].astype(BF16)
    width = gate_ref.shape[-1]
    n_chunks = q_ref.shape[1]
    cw = q_ref.shape[-1]
    for part, ref in enumerate((q_ref, k_ref, v_ref)):
        r = jnp.dot(xb, w_ref[:, part * width:(part + 1) * width], preferred_element_type=F32)
        if part == 0:
            r = r * (NA_SCALE * LOG2_E)
        r = r.astype(BF16)
        for c in range(n_chunks):
            ref[0, c] = r[:, c * cw:(c + 1) * cw]
    gate_ref[0] = jnp.dot(xb, w_ref[:, 3 * width:4 * width], preferred_element_type=F32)


def _na_inproj(x, w_in, *, tm=512):
    bsz, seq, d = x.shape
    width = w_in.shape[1] // 4
    n_chunks = width // V7X_MXU_DIM
    chunked = jax.ShapeDtypeStruct((bsz, n_chunks, seq, V7X_MXU_DIM), BF16)
    chunk_spec = pl.BlockSpec((1, n_chunks, tm, V7X_MXU_DIM), lambda b, j: (b, 0, j, 0))
    return pl.pallas_call(
        _na_inproj_kernel,
        out_shape=(chunked, chunked, chunked, jax.ShapeDtypeStruct((bsz, seq, width), F32)),
        grid=(bsz, seq // tm),
        in_specs=[pl.BlockSpec((1, tm, d), lambda b, j: (b, j, 0)), _resident(w_in.shape)],
        out_specs=(chunk_spec, chunk_spec, chunk_spec, pl.BlockSpec((1, tm, width), lambda b, j: (b, j, 0))),
        compiler_params=pltpu.CompilerParams(
            dimension_semantics=("arbitrary", "arbitrary"),
            vmem_limit_bytes=56 * 1024 * 1024),
        name="na_inproj",
    )(x, w_in)


def _kv_window_start(blk, rows):
    return jnp.clip(blk * ROWS_PER_STEP - HALO_ROWS, 0, rows - (ROWS_PER_STEP + 2 * HALO_ROWS))


def _na_layer_kernel(q_ref, kbuf, vbuf, gate_ref, x_ref, tab_ref, wout_ref, g_ref, b_ref, o_ref, obuf, *, rows):
    blk = pl.program_id(1)
    n_chunks = q_ref.shape[1]
    cw = q_ref.shape[-1]
    window_row0 = _kv_window_start(blk, rows)

    def head_masks_for(n_rows):
        lane_head = lax.broadcasted_iota(jnp.int32, (n_rows, cw), 1) // HEAD_DIM
        return [lane_head == h for h in range(HEADS_PER_CHUNK)]

    head_masks = {nq: head_masks_for(nq) for nq in sorted({q1 - q0 for q0, q1, _ in QUERY_GROUPS})}
    n_blocks = WIN_H // KEY_ROWS_PER_BLOCK

    def row_geometry(i):
        r = blk * ROWS_PER_STEP + i
        rs = jnp.clip(r - HALO_ROWS, 0, rows - WIN_H)
        return i * GRID_W, (rs - window_row0) * GRID_W, r - rs

    def scores(geom, c, group):
        qoff, koff, row_off = geom
        q0, q1, k0 = group
        nq = q1 - q0
        qf = q_ref[0, c, qoff:qoff + GRID_W, :].astype(F32)
        starts = [pl.multiple_of(koff + kr * GRID_W + k0, KEY_COL_ALIGN) for kr in range(WIN_H)]
        kg = jnp.concatenate([kbuf[0, c, pl.ds(s0, KEY_COLS), :] for s0 in starts], axis=0)
        qs = jnp.concatenate([jnp.where(m, qf[q0:q1], 0.0) for m in head_masks[nq]],
                             axis=0).astype(BF16)
        s = lax.dot_general(qs, kg, (((1,), (1,)), ((), ())), preferred_element_type=F32)
        biased = []
        for h in range(HEADS_PER_CHUNK):
            bias = jnp.concatenate(
                [tab_ref[c * HEADS_PER_CHUNK + h, (WIN_H - 1) - row_off + KEY_ROWS_PER_BLOCK * m, q0:q1, :]
                 for m in range(n_blocks)], axis=1)
            biased.append(s[h * nq:(h + 1) * nq] + bias)
        s = jnp.concatenate(biased, axis=0)
        return (s, jnp.max(s, axis=-1, keepdims=True)), starts

    def finish(geom, c, group, s_and_max, starts):
        qoff = geom[0]
        q0, q1, _ = group
        nq = q1 - q0
        masks = head_masks[nq]
        s, s_max = s_and_max
        p = jnp.exp2(s - s_max)
        l = jnp.sum(p, axis=-1, keepdims=True)
        vg = jnp.concatenate([vbuf[0, c, pl.ds(s0, KEY_COLS), :] for s0 in starts], axis=0)
        pv = jnp.dot(p.astype(BF16), vg, preferred_element_type=F32)
        last = HEADS_PER_CHUNK - 1
        oc = pv[last * nq:]
        lc = jnp.broadcast_to(l[last * nq:], (nq, cw))
        for h in range(last - 1, -1, -1):
            oc = jnp.where(masks[h], pv[h * nq:(h + 1) * nq], oc)
            lc = jnp.where(masks[h], l[h * nq:(h + 1) * nq], lc)
        obuf[c, qoff + q0:qoff + q1, :] = oc * (1.0 / lc)

    def epilogue_pieces(row0, row1):
        tok = slice(row0 * GRID_W, row1 * GRID_W)
        state = {}

        def gate():
            o_part = jnp.concatenate([obuf[c, tok, :] for c in range(n_chunks)], axis=1)
            state["a"] = (o_part * _silu(gate_ref[0, tok, :])).astype(BF16)

        def project(nb):
            state[nb] = jnp.dot(state["a"], wout_ref[:, nb * cw:(nb + 1) * cw], preferred_element_type=F32)

        def normalise():
            h2 = jnp.concatenate([state[nb] for nb in range(n_chunks)], axis=1)
            z = DEEPNORM_ALPHA * x_ref[0, tok, :] + h2
            o_ref[0, tok, :] = _layer_norm(z, g_ref[...], b_ref[...])

        return [gate] + [functools.partial(project, nb) for nb in range(n_chunks)] + [normalise]

    units = [(i, c, g) for i in range(ROWS_PER_STEP) for c in range(n_chunks) for g in QUERY_GROUPS]
    geoms = [row_geometry(i) for i in range(ROWS_PER_STEP)]
    units_per_row = n_chunks * len(QUERY_GROUPS)
    part_ends = np.cumsum(EPILOGUE_ROW_SPLITS)
    assert part_ends[-1] == ROWS_PER_STEP
    pending, epilogue = [], []
    n_finished = 0
    for n_scored in range(len(units) + SCORE_LOOKAHEAD):
        if n_scored < len(units):
            i, c, g = units[n_scored]
            pending.append(((geoms[i], c, g), scores(geoms[i], c, g)))
        if n_scored >= SCORE_LOOKAHEAD:
            done, (s, starts) = pending.pop(0)
            finish(*done, s, starts)
            n_finished += 1
            if epilogue:
                epilogue.pop(0)()
            if n_finished % units_per_row == 0 and n_finished // units_per_row in part_ends:
                row1 = n_finished // units_per_row
                row0 = row1 - EPILOGUE_ROW_SPLITS[list(part_ends).index(row1)]
                epilogue += epilogue_pieces(row0, row1)
    for piece in epilogue:
        piece()


def _bias_table_kernel(y_ref, o_ref):
    q = lax.broadcasted_iota(jnp.int32, (GRID_W, V7X_LANES), 0)
    kcc = lax.broadcasted_iota(jnp.int32, (GRID_W, V7X_LANES), 1) % KEY_COLS
    cs = jnp.clip(q - WIN_W // 2, 0, GRID_W - WIN_W)
    k0 = jnp.zeros_like(q)
    for q0, _, first_key in QUERY_GROUPS:
        k0 = jnp.where(q >= q0, first_key, k0)
    kc = k0 + kcc
    valid = (kc >= cs) & (kc < cs + WIN_W)
    for d in range(o_ref.shape[1]):
        yb = jnp.broadcast_to(y_ref[0, d:d + 1, :], (GRID_W, V7X_LANES))
        t = jnp.zeros((GRID_W, V7X_LANES), F32)
        for q0, _, first_key in QUERY_GROUPS:
            shift = (-(first_key + WIN_W - 1)) % V7X_LANES
            rolled = pltpu.roll(yb, shift, 1, stride=1, stride_axis=0)
            t = jnp.where(q >= q0, rolled, t)
        o_ref[0, d] = jnp.where(valid, t * LOG2_E, MASK_VALUE)


def _bias_table(rpb):
    n_heads = rpb.shape[0]
    n_d = 2 * WIN_H - KEY_ROWS_PER_BLOCK
    padded = jnp.pad(rpb.astype(F32), ((0, 0), (0, 0), (0, KEY_COLS - rpb.shape[2])))
    y = jnp.stack([padded[:, d:d + KEY_ROWS_PER_BLOCK].reshape(n_heads, V7X_LANES) for d in range(n_d)], axis=1)
    return pl.pallas_call(
        _bias_table_kernel,
        out_shape=jax.ShapeDtypeStruct((n_heads, n_d, GRID_W, V7X_LANES), F32),
        grid=(n_heads,),
        in_specs=[pl.BlockSpec((1, n_d, V7X_LANES), lambda h: (h, 0, 0))],
        out_specs=pl.BlockSpec((1, n_d, GRID_W, V7X_LANES), lambda h: (h, 0, 0, 0)),
        compiler_params=pltpu.CompilerParams(dimension_semantics=("arbitrary",)),
        name="bias_table",
    )(y)


def _na_layer(q, k, v, gate, x, tab, w_out, ln_g, ln_b):
    bsz, seq, d = x.shape
    rows = seq // GRID_W
    n_chunks, cw = q.shape[1], q.shape[3]
    main = ROWS_PER_STEP * GRID_W
    window = (ROWS_PER_STEP + 2 * HALO_ROWS) * GRID_W
    assert rows % ROWS_PER_STEP == 0 and rows * GRID_W >= window
    main_spec = pl.BlockSpec((1, n_chunks, main, cw), lambda b, j: (b, 0, j, 0))
    kv_spec = pl.BlockSpec((pl.Element(1), pl.Element(n_chunks), pl.Element(window), pl.Element(cw)),
                           lambda b, j: (b, 0, _kv_window_start(j, rows) * GRID_W, 0))
    tok_spec = pl.BlockSpec((1, main, d), lambda b, j: (b, j, 0))
    kernel = functools.partial(_na_layer_kernel, rows=rows)
    return pl.pallas_call(
        kernel,
        out_shape=jax.ShapeDtypeStruct(x.shape, F32),
        grid=(bsz, rows // ROWS_PER_STEP),
        in_specs=[main_spec, kv_spec, kv_spec, tok_spec, tok_spec, _resident(tab.shape), _resident(w_out.shape),
                  _resident(ln_g.shape), _resident(ln_b.shape)],
        out_specs=tok_spec,
        scratch_shapes=[pltpu.VMEM((n_chunks, main, cw), F32)],
        compiler_params=pltpu.CompilerParams(
            dimension_semantics=("arbitrary", "arbitrary"),
            vmem_limit_bytes=60 * 1024 * 1024),
        name="na_layer",
    )(q, k, v, gate, x, tab, w_out, ln_g, ln_b)


def _trunk(x, p):
    x1 = _pool_layer(x, p["w_in_pool"], p["w_out_pool"], p["ln_g0"], p["ln_b0"])
    q, k, v, gate = _na_inproj(x1, p["w_in_na"])
    return _na_layer(q, k, v, gate, x1, p["tab"], p["w_out_na"], p["ln_g1"], p["ln_b1"])


def kernel(x_prompt, x_sample, w_in_pool, w_grp_pool, scale_pool, w_out_pool, w_in_na, rpb_na, w_out_na, ln_g, ln_b):
    scale = scale_pool[0].reshape(1, -1).astype(F32)
    p = {
        "w_in_pool": _fold_pool_weights(w_in_pool[0], w_grp_pool[0], scale),
        "w_out_pool": w_out_pool[0].astype(BF16),
        "w_in_na": w_in_na[0].astype(BF16),
        "w_out_na": w_out_na[0].astype(BF16),
        "tab": _bias_table(rpb_na[0]),
        "ln_g0": ln_g[0].reshape(1, -1).astype(F32),
        "ln_b0": ln_b[0].reshape(1, -1).astype(F32),
        "ln_g1": ln_g[1].reshape(1, -1).astype(F32),
        "ln_b1": ln_b[1].reshape(1, -1).astype(F32),
    }
    return (_trunk(x_prompt, p), _trunk(x_sample, p))
```

```python
import functools

import jax
import jax.numpy as jnp
import numpy as np
from jax import lax
from jax.experimental import pallas as pl
from jax.experimental.pallas import tpu as pltpu

F32 = jnp.float32
BF16 = jnp.bfloat16

GRID_W = 64
POOL_WINDOWS = (2, 4, 8, 16)
HEAD_DIM = 32
WIN_H = 8
WIN_W = 16
NA_SCALE = HEAD_DIM ** -0.5
LOG2_E = 1.4426950408889634
LN_EPS = 1e-5
DEPTH = 2
DEEPNORM_ALPHA = (2 * DEPTH) ** 0.25

V7X_SUBLANES_F32 = 8
V7X_LANES = 128
V7X_MXU_DIM = 256
V7X_VMEM_BYTES = 64 * 1024 * 1024
V7X_VMEM_RESERVE_BYTES = 4 * 1024 * 1024

MASK_VALUE = -1e30
POOL_HALO = max(POOL_WINDOWS) // 2
HEADS_PER_CHUNK = V7X_MXU_DIM // HEAD_DIM
ROWS_PER_STEP = 8
HALO_ROWS = WIN_H // 2
V7X_SUBLANES_BF16 = 16
KEY_COLS = 2 * WIN_W
KEY_COL_ALIGN = V7X_SUBLANES_BF16
KEY_ROWS_PER_BLOCK = V7X_LANES // KEY_COLS


def _query_groups():
    groups = []
    for q in range(GRID_W):
        cs = min(max(q - WIN_W // 2, 0), GRID_W - WIN_W)
        k0 = min(cs // KEY_COL_ALIGN * KEY_COL_ALIGN, GRID_W - KEY_COLS)
        assert k0 <= cs and cs + WIN_W <= k0 + KEY_COLS
        if groups and groups[-1][2] == k0:
            groups[-1][1] = q + 1
        else:
            groups.append([q, q + 1, k0])
    assert all(g[0] % V7X_SUBLANES_F32 == 0 for g in groups)
    return tuple(tuple(g) for g in groups)


QUERY_GROUPS = _query_groups()
SCORE_LOOKAHEAD = 4
POOL_TAIL_CHUNKS = 4
POOL_SUBTILES = 1
EPILOGUE_ROW_SPLITS = (6, 2)


def _silu(x):
    return x / (1.0 + jnp.exp(-x))


def _layer_norm(z, g, b):
    mu = jnp.mean(z, axis=-1, keepdims=True)
    zc = z - mu
    var = jnp.mean(zc * zc, axis=-1, keepdims=True)
    return zc * lax.rsqrt(var + LN_EPS) * g + b


def _fold_pool_weights_kernel(wu_ref, wgate_ref, wgrp_ref, scale_ref, o_ref):
    gw = wu_ref.shape[1]
    folded = jnp.dot(wu_ref[...], wgrp_ref[0], preferred_element_type=F32, precision=lax.Precision.HIGHEST)
    o_ref[:, :gw] = (folded * scale_ref[...]).astype(o_ref.dtype)
    o_ref[:, gw:] = wgate_ref[...].astype(o_ref.dtype)


def _fold_pool_weights(w_in, w_grp, scale):
    d = w_in.shape[0]
    n_groups, gw, _ = w_grp.shape
    return pl.pallas_call(
        _fold_pool_weights_kernel,
        out_shape=jax.ShapeDtypeStruct((d, 2 * n_groups * gw), BF16),
        grid=(n_groups,),
        in_specs=[pl.BlockSpec((d, gw), lambda g: (0, g)),
                  pl.BlockSpec((d, gw), lambda g: (0, n_groups + g)),
                  pl.BlockSpec((1, gw, gw), lambda g: (g, 0, 0)),
                  pl.BlockSpec((1, gw), lambda g: (0, g))],
        out_specs=pl.BlockSpec((d, 2 * gw), lambda g: (0, g)),
        compiler_params=pltpu.CompilerParams(dimension_semantics=("arbitrary",)),
        name="fold_pool_weights",
    )(w_in, w_in, w_grp, scale)


def _pool_layer_kernel(xp_ref, x_ref, xn_ref, win_ref, wout_ref, g_ref, b_ref, o_ref, xe_ref, *, seq, ts):
    j = pl.program_id(1)
    nj = pl.num_programs(1)
    x = x_ref[0]
    prev = jnp.where(j > 0, xp_ref[0], 0.0)
    nxt = jnp.where(j < nj - 1, xn_ref[0], 0.0)
    xe_ref[...] = jnp.concatenate([prev, x, nxt], axis=0).astype(BF16)

    tsub = ts // POOL_SUBTILES
    n_ext = tsub + 2 * POOL_HALO
    n_groups = len(POOL_WINDOWS)
    gw = win_ref.shape[1] // (2 * n_groups)

    def project(sub, g):
        r0 = sub * tsub
        r = jnp.dot(xe_ref[r0:r0 + n_ext], win_ref[:, 2 * g * gw:2 * (g + 1) * gw],
                    preferred_element_type=F32)
        return r[:, :gw], r[POOL_HALO:POOL_HALO + tsub, gw:]

    def mix(sub, g, u, gate):
        h = POOL_WINDOWS[g] // 2
        b = u + pltpu.roll(u, 1, 0)
        span = 2
        while span < h:
            b = b + pltpu.roll(b, span, 0)
            span *= 2
        if h == 1:
            wsum = b
        else:
            wsum = pltpu.roll(b, n_ext - (h - 1), 0) + pltpu.roll(b, 1, 0)
        t = j * ts + sub * tsub + lax.broadcasted_iota(jnp.int32, (tsub, 1), 0)
        cnt = (jnp.minimum(t + h, seq) - jnp.maximum(t - h, 0)).astype(F32)
        y = wsum[POOL_HALO:POOL_HALO + tsub] * (1.0 / cnt) - u[POOL_HALO:POOL_HALO + tsub]
        return (y * _silu(gate)).astype(BF16)

    units = [(sub, g) for sub in range(POOL_SUBTILES) for g in range(n_groups)]
    ahead = project(*units[0])
    acc = None
    for n, (sub, g) in enumerate(units):
        cur = ahead
        if n + 1 < len(units):
            ahead = project(*units[n + 1])
        a = mix(sub, g, *cur)
        w_g = wout_ref[g * gw:(g + 1) * gw, :]
        if g < n_groups - 1:
            part = jnp.dot(a, w_g, preferred_element_type=F32)
            acc = part if g == 0 else acc + part
            continue
        rc = tsub // POOL_TAIL_CHUNKS
        for k in range(POOL_TAIL_CHUNKS):
            lo = k * rc
            h = acc[lo:lo + rc] + jnp.dot(a[lo:lo + rc], w_g, preferred_element_type=F32)
            rows_ = slice(sub * tsub + lo, sub * tsub + lo + rc)
            z = DEEPNORM_ALPHA * x_ref[0, rows_, :] + h
            o_ref[0, rows_, :] = _layer_norm(z, g_ref[...], b_ref[...])


def _resident(shape):
    nd = len(shape)
    return pl.BlockSpec(shape, lambda *_: (0,) * nd, pipeline_mode=pl.Buffered(1))


def _nbytes(shape, dtype):
    return int(np.prod(shape)) * jnp.dtype(dtype).itemsize


def _vmem_limit(resident, pipelined, scratch, temporaries):
    return min(resident + 2 * pipelined + scratch + temporaries, V7X_VMEM_BYTES - V7X_VMEM_RESERVE_BYTES)


def _pool_layer(x, w_in, w_out, ln_g, ln_b, *, ts=1024):
    bsz, seq, d = x.shape
    assert seq % ts == 0 and ts % POOL_HALO == 0
    n_halo_blocks = seq // POOL_HALO
    per_tile = ts // POOL_HALO
    kernel = functools.partial(_pool_layer_kernel, seq=seq, ts=ts)
    tile_f32 = _nbytes((ts, d), F32)
    vmem_limit = _vmem_limit(
        resident=_nbytes(w_in.shape, w_in.dtype) + _nbytes(w_out.shape, w_out.dtype),
        pipelined=2 * tile_f32 + 2 * _nbytes((POOL_HALO, d), F32),
        scratch=_nbytes((ts + 2 * POOL_HALO, d), BF16),
        temporaries=6 * tile_f32)
    return pl.pallas_call(
        kernel,
        out_shape=jax.ShapeDtypeStruct(x.shape, F32),
        grid=(bsz, seq // ts),
        in_specs=[
            pl.BlockSpec((1, POOL_HALO, d), lambda b, j: (b, jnp.maximum(j * per_tile - 1, 0), 0)),
            pl.BlockSpec((1, ts, d), lambda b, j: (b, j, 0)),
            pl.BlockSpec((1, POOL_HALO, d), lambda b, j: (b, jnp.minimum((j + 1) * per_tile, n_halo_blocks - 1), 0)),
            _resident(w_in.shape), _resident(w_out.shape), _resident(ln_g.shape), _resident(ln_b.shape),
        ],
        out_specs=pl.BlockSpec((1, ts, d), lambda b, j: (b, j, 0)),
        scratch_shapes=[pltpu.VMEM((ts + 2 * POOL_HALO, d), BF16)],
        compiler_params=pltpu.CompilerParams(
            dimension_semantics=("arbitrary", "arbitrary"),
            vmem_limit_bytes=vmem_limit),
        name="pool_layer",
    )(x, x, x, w_in, w_out, ln_g, ln_b)


def _na_inproj_kernel(x_ref, w_ref, q_ref, k_ref, v_ref, gate_ref):
    xb = x_ref[0].astype(BF16)
    width = gate_ref.shape[-1]
    n_chunks = q_ref.shape[1]
    cw = q_ref.shape[-1]
    full = jnp.dot(xb, w_ref[...], preferred_element_type=F32)
    for part, ref in enumerate((q_ref, k_ref, v_ref)):
        r = full[:, part * width:(part + 1) * width]
        if part == 0:
            r = r * (NA_SCALE * LOG2_E)
        r = r.astype(BF16)
        for c in range(n_chunks):
            ref[0, c] = r[:, c * cw:(c + 1) * cw]
    gate_ref[0] = full[:, 3 * width:4 * width]


def _na_inproj(x, w_in, *, tm=512):
    bsz, seq, d = x.shape
    width = w_in.shape[1] // 4
    n_chunks = width // V7X_MXU_DIM
    chunked = jax.ShapeDtypeStruct((bsz, n_chunks, seq, V7X_MXU_DIM), BF16)
    chunk_spec = pl.BlockSpec((1, n_chunks, tm, V7X_MXU_DIM), lambda b, j: (b, 0, j, 0))
    vmem_limit = _vmem_limit(
        resident=_nbytes(w_in.shape, w_in.dtype),
        pipelined=_nbytes((tm, d), F32) + 3 * _nbytes((tm, width), BF16) + _nbytes((tm, width), F32),
        scratch=0,
        temporaries=2 * _nbytes((tm, 4 * width), F32))
    return pl.pallas_call(
        _na_inproj_kernel,
        out_shape=(chunked, chunked, chunked, jax.ShapeDtypeStruct((bsz, seq, width), F32)),
        grid=(bsz, seq // tm),
        in_specs=[pl.BlockSpec((1, tm, d), lambda b, j: (b, j, 0)), _resident(w_in.shape)],
        out_specs=(chunk_spec, chunk_spec, chunk_spec, pl.BlockSpec((1, tm, width), lambda b, j: (b, j, 0))),
        compiler_params=pltpu.CompilerParams(
            dimension_semantics=("arbitrary", "arbitrary"),
            vmem_limit_bytes=vmem_limit),
        name="na_inproj",
    )(x, w_in)


def _kv_window_start(blk, rows):
    return jnp.clip(blk * ROWS_PER_STEP - HALO_ROWS, 0, rows - (ROWS_PER_STEP + 2 * HALO_ROWS))


def _na_layer_kernel(q_ref, kbuf, vbuf, gate_ref, x_ref, tab_ref, wout_ref, g_ref, b_ref, o_ref, obuf, *, rows):
    blk = pl.program_id(1)
    n_chunks = q_ref.shape[1]
    cw = q_ref.shape[-1]
    window_row0 = _kv_window_start(blk, rows)

    def head_masks_for(n_rows):
        lane_head = lax.broadcasted_iota(jnp.int32, (n_rows, cw), 1) // HEAD_DIM
        return [lane_head == h for h in range(HEADS_PER_CHUNK)]

    head_masks = {nq: head_masks_for(nq) for nq in sorted({q1 - q0 for q0, q1, _ in QUERY_GROUPS})}
    n_blocks = WIN_H // KEY_ROWS_PER_BLOCK

    def row_geometry(i):
        r = blk * ROWS_PER_STEP + i
        rs = jnp.clip(r - HALO_ROWS, 0, rows - WIN_H)
        return i * GRID_W, (rs - window_row0) * GRID_W, r - rs

    def scores(geom, c, group):
        qoff, koff, row_off = geom
        q0, q1, k0 = group
        nq = q1 - q0
        qf = q_ref[0, c, qoff:qoff + GRID_W, :].astype(F32)
        starts = [pl.multiple_of(koff + kr * GRID_W + k0, KEY_COL_ALIGN) for kr in range(WIN_H)]
        kg = jnp.concatenate([kbuf[0, c, pl.ds(s0, KEY_COLS), :] for s0 in starts], axis=0)
        qs = jnp.concatenate([jnp.where(m, qf[q0:q1], 0.0) for m in head_masks[nq]],
                             axis=0).astype(BF16)
        s = lax.dot_general(qs, kg, (((1,), (1,)), ((), ())), preferred_element_type=F32)
        biased = []
        for h in range(HEADS_PER_CHUNK):
            bias = jnp.concatenate(
                [tab_ref[c * HEADS_PER_CHUNK + h, (WIN_H - 1) - row_off + KEY_ROWS_PER_BLOCK * m, q0:q1, :]
                 for m in range(n_blocks)], axis=1)
            biased.append(s[h * nq:(h + 1) * nq] + bias)
        s = jnp.concatenate(biased, axis=0)
        return (s, jnp.max(s, axis=-1, keepdims=True)), starts

    def finish(geom, c, group, s_and_max, starts):
        qoff = geom[0]
        q0, q1, _ = group
        nq = q1 - q0
        masks = head_masks[nq]
        s, s_max = s_and_max
        p = jnp.exp2(s - s_max)
        l = jnp.sum(p, axis=-1, keepdims=True)
        vg = jnp.concatenate([vbuf[0, c, pl.ds(s0, KEY_COLS), :] for s0 in starts], axis=0)
        pv = jnp.dot(p.astype(BF16), vg, preferred_element_type=F32)
        last = HEADS_PER_CHUNK - 1
        oc = pv[last * nq:]
        lc = jnp.broadcast_to(l[last * nq:], (nq, cw))
        for h in range(last - 1, -1, -1):
            oc = jnp.where(masks[h], pv[h * nq:(h + 1) * nq], oc)
            lc = jnp.where(masks[h], l[h * nq:(h + 1) * nq], lc)
        obuf[c, qoff + q0:qoff + q1, :] = oc * (1.0 / lc)

    def epilogue_pieces(row0, row1):
        tok = slice(row0 * GRID_W, row1 * GRID_W)
        state = {}

        def gate():
            o_part = jnp.concatenate([obuf[c, tok, :] for c in range(n_chunks)], axis=1)
            state["a"] = (o_part * _silu(gate_ref[0, tok, :])).astype(BF16)

        def project(nb):
            state[nb] = jnp.dot(state["a"], wout_ref[:, nb * cw:(nb + 1) * cw], preferred_element_type=F32)

        def normalise():
            h2 = jnp.concatenate([state[nb] for nb in range(n_chunks)], axis=1)
            z = DEEPNORM_ALPHA * x_ref[0, tok, :] + h2
            o_ref[0, tok, :] = _layer_norm(z, g_ref[...], b_ref[...])

        return [gate] + [functools.partial(project, nb) for nb in range(n_chunks)] + [normalise]

    units = [(i, c, g) for i in range(ROWS_PER_STEP) for c in range(n_chunks) for g in QUERY_GROUPS]
    geoms = [row_geometry(i) for i in range(ROWS_PER_STEP)]
    units_per_row = n_chunks * len(QUERY_GROUPS)
    part_ends = np.cumsum(EPILOGUE_ROW_SPLITS)
    assert part_ends[-1] == ROWS_PER_STEP
    pending, epilogue = [], []
    n_finished = 0
    for n_scored in range(len(units) + SCORE_LOOKAHEAD):
        if n_scored < len(units):
            i, c, g = units[n_scored]
            pending.append(((geoms[i], c, g), scores(geoms[i], c, g)))
        if n_scored >= SCORE_LOOKAHEAD:
            done, (s, starts) = pending.pop(0)
            finish(*done, s, starts)
            n_finished += 1
            if epilogue:
                epilogue.pop(0)()
            if n_finished % units_per_row == 0 and n_finished // units_per_row in part_ends:
                row1 = n_finished // units_per_row
                row0 = row1 - EPILOGUE_ROW_SPLITS[list(part_ends).index(row1)]
                epilogue += epilogue_pieces(row0, row1)
    for piece in epilogue:
        piece()


def _bias_table_kernel(y_ref, o_ref):
    q = lax.broadcasted_iota(jnp.int32, (GRID_W, V7X_LANES), 0)
    kcc = lax.broadcasted_iota(jnp.int32, (GRID_W, V7X_LANES), 1) % KEY_COLS
    cs = jnp.clip(q - WIN_W // 2, 0, GRID_W - WIN_W)
    k0 = jnp.zeros_like(q)
    for q0, _, first_key in QUERY_GROUPS:
        k0 = jnp.where(q >= q0, first_key, k0)
    kc = k0 + kcc
    valid = (kc >= cs) & (kc < cs + WIN_W)
    for d in range(o_ref.shape[1]):
        yb = jnp.broadcast_to(y_ref[0, d:d + 1, :], (GRID_W, V7X_LANES))
        t = jnp.zeros((GRID_W, V7X_LANES), F32)
        for q0, _, first_key in QUERY_GROUPS:
            shift = (-(first_key + WIN_W - 1)) % V7X_LANES
            rolled = pltpu.roll(yb, shift, 1, stride=1, stride_axis=0)
            t = jnp.where(q >= q0, rolled, t)
        o_ref[0, d] = jnp.where(valid, t * LOG2_E, MASK_VALUE)


def _bias_table(rpb):
    n_heads = rpb.shape[0]
    n_d = 2 * WIN_H - KEY_ROWS_PER_BLOCK
    padded = jnp.pad(rpb.astype(F32), ((0, 0), (0, 0), (0, KEY_COLS - rpb.shape[2])))
    y = jnp.stack([padded[:, d:d + KEY_ROWS_PER_BLOCK].reshape(n_heads, V7X_LANES) for d in range(n_d)], axis=1)
    return pl.pallas_call(
        _bias_table_kernel,
        out_shape=jax.ShapeDtypeStruct((n_heads, n_d, GRID_W, V7X_LANES), F32),
        grid=(n_heads,),
        in_specs=[pl.BlockSpec((1, n_d, V7X_LANES), lambda h: (h, 0, 0))],
        out_specs=pl.BlockSpec((1, n_d, GRID_W, V7X_LANES), lambda h: (h, 0, 0, 0)),
        compiler_params=pltpu.CompilerParams(dimension_semantics=("arbitrary",)),
        name="bias_table",
    )(y)


def _na_layer(q, k, v, gate, x, tab, w_out, ln_g, ln_b):
    bsz, seq, d = x.shape
    rows = seq // GRID_W
    n_chunks, cw = q.shape[1], q.shape[3]
    main = ROWS_PER_STEP * GRID_W
    window = (ROWS_PER_STEP + 2 * HALO_ROWS) * GRID_W
    assert rows % ROWS_PER_STEP == 0 and rows * GRID_W >= window
    main_spec = pl.BlockSpec((1, n_chunks, main, cw), lambda b, j: (b, 0, j, 0))
    kv_spec = pl.BlockSpec((pl.Element(1), pl.Element(n_chunks), pl.Element(window), pl.Element(cw)),
                           lambda b, j: (b, 0, _kv_window_start(j, rows) * GRID_W, 0))
    tok_spec = pl.BlockSpec((1, main, d), lambda b, j: (b, j, 0))
    kernel = functools.partial(_na_layer_kernel, rows=rows)
    tile_f32 = _nbytes((main, d), F32)
    vmem_limit = _vmem_limit(
        resident=_nbytes(tab.shape, tab.dtype) + _nbytes(w_out.shape, w_out.dtype),
        pipelined=_nbytes((main, d), BF16) + 2 * _nbytes((window, d), BF16) + 3 * tile_f32,
        scratch=tile_f32,
        temporaries=4 * tile_f32)
    return pl.pallas_call(
        kernel,
        out_shape=jax.ShapeDtypeStruct(x.shape, F32),
        grid=(bsz, rows // ROWS_PER_STEP),
        in_specs=[main_spec, kv_spec, kv_spec, tok_spec, tok_spec, _resident(tab.shape), _resident(w_out.shape),
                  _resident(ln_g.shape), _resident(ln_b.shape)],
        out_specs=tok_spec,
        scratch_shapes=[pltpu.VMEM((n_chunks, main, cw), F32)],
        compiler_params=pltpu.CompilerParams(
            dimension_semantics=("arbitrary", "arbitrary"),
            vmem_limit_bytes=vmem_limit),
        name="na_layer",
    )(q, k, v, gate, x, tab, w_out, ln_g, ln_b)


def _trunk(x, p):
    x1 = _pool_layer(x, p["w_in_pool"], p["w_out_pool"], p["ln_g0"], p["ln_b0"])
    q, k, v, gate = _na_inproj(x1, p["w_in_na"])
    return _na_layer(q, k, v, gate, x1, p["tab"], p["w_out_na"], p["ln_g1"], p["ln_b1"])


def kernel(x_prompt, x_sample, w_in_pool, w_grp_pool, scale_pool, w_out_pool, w_in_na, rpb_na, w_out_na, ln_g, ln_b):
    scale = scale_pool[0].reshape(1, -1).astype(F32)
    p = {
        "w_in_pool": _fold_pool_weights(w_in_pool[0], w_grp_pool[0], scale),
        "w_out_pool": w_out_pool[0].astype(BF16),
        "w_in_na": w_in_na[0].astype(BF16),
        "w_out_na": w_out_na[0].astype(BF16),
        "tab": _bias_table(rpb_na[0]),
        "ln_g0": ln_g[0].reshape(1, -1).astype(F32),
        "ln_b0": ln_b[0].reshape(1, -1).astype(F32),
        "ln_g1": ln_g[1].reshape(1, -1).astype(F32),
        "ln_b1": ln_b[1].reshape(1, -1).astype(F32),
    }
    return (_trunk(x_prompt, p), _trunk(x_sample, p))
```

```python
import functools

import jax
import jax.numpy as jnp
import numpy as np
from jax import lax
from jax.experimental import pallas as pl
from jax.experimental.pallas import tpu as pltpu

F32 = jnp.float32
BF16 = jnp.bfloat16

GRID_W = 64
POOL_WINDOWS = (2, 4, 8, 16)
HEAD_DIM = 32
WIN_H = 8
WIN_W = 16
NA_SCALE = HEAD_DIM ** -0.5
LOG2_E = 1.4426950408889634
LN_EPS = 1e-5
DEPTH = 2
DEEPNORM_ALPHA = (2 * DEPTH) ** 0.25

V7X_SUBLANES_F32 = 8
V7X_LANES = 128
V7X_MXU_DIM = 256
V7X_VMEM_BYTES = 64 * 1024 * 1024
V7X_VMEM_RESERVE_BYTES = 4 * 1024 * 1024

MASK_VALUE = -1e30
POOL_HALO = max(POOL_WINDOWS) // 2
HEADS_PER_CHUNK = V7X_MXU_DIM // HEAD_DIM
ROWS_PER_STEP = 8
HALO_ROWS = WIN_H // 2
V7X_SUBLANES_BF16 = 16
KEY_COLS = 2 * WIN_W
KEY_COL_ALIGN = V7X_SUBLANES_BF16
KEY_ROWS_PER_BLOCK = V7X_LANES // KEY_COLS


def _query_groups():
    groups = []
    for q in range(GRID_W):
        cs = min(max(q - WIN_W // 2, 0), GRID_W - WIN_W)
        k0 = min(cs // KEY_COL_ALIGN * KEY_COL_ALIGN, GRID_W - KEY_COLS)
        assert k0 <= cs and cs + WIN_W <= k0 + KEY_COLS
        if groups and groups[-1][2] == k0:
            groups[-1][1] = q + 1
        else:
            groups.append([q, q + 1, k0])
    assert all(g[0] % V7X_SUBLANES_F32 == 0 for g in groups)
    return tuple(tuple(g) for g in groups)


QUERY_GROUPS = _query_groups()
SCORE_LOOKAHEAD = 3
POOL_TAIL_CHUNKS = 4
POOL_SUBTILES = 1
EPILOGUE_ROW_SPLITS = (6, 2)


def _silu(x):
    return x / (1.0 + jnp.exp(-x))


def _layer_norm(z, g, b):
    mu = jnp.mean(z, axis=-1, keepdims=True)
    zc = z - mu
    var = jnp.mean(zc * zc, axis=-1, keepdims=True)
    return zc * lax.rsqrt(var + LN_EPS) * g + b


def _fold_pool_weights_kernel(wu_ref, wgate_ref, wgrp_ref, scale_ref, o_ref):
    gw = wu_ref.shape[1]
    folded = jnp.dot(wu_ref[...], wgrp_ref[0], preferred_element_type=F32, precision=lax.Precision.HIGHEST)
    o_ref[:, :gw] = (folded * scale_ref[...]).astype(o_ref.dtype)
    o_ref[:, gw:] = wgate_ref[...].astype(o_ref.dtype)


def _fold_pool_weights(w_in, w_grp, scale):
    d = w_in.shape[0]
    n_groups, gw, _ = w_grp.shape
    return pl.pallas_call(
        _fold_pool_weights_kernel,
        out_shape=jax.ShapeDtypeStruct((d, 2 * n_groups * gw), BF16),
        grid=(n_groups,),
        in_specs=[pl.BlockSpec((d, gw), lambda g: (0, g)),
                  pl.BlockSpec((d, gw), lambda g: (0, n_groups + g)),
                  pl.BlockSpec((1, gw, gw), lambda g: (g, 0, 0)),
                  pl.BlockSpec((1, gw), lambda g: (0, g))],
        out_specs=pl.BlockSpec((d, 2 * gw), lambda g: (0, g)),
        compiler_params=pltpu.CompilerParams(dimension_semantics=("arbitrary",)),
        name="fold_pool_weights",
    )(w_in, w_in, w_grp, scale)


def _pool_layer_kernel(xp_ref, x_ref, xn_ref, win_ref, wout_ref, g_ref, b_ref, o_ref, xe_ref, *, seq, ts):
    j = pl.program_id(1)
    nj = pl.num_programs(1)
    x = x_ref[0]
    prev = jnp.where(j > 0, xp_ref[0], 0.0)
    nxt = jnp.where(j < nj - 1, xn_ref[0], 0.0)
    xe_ref[...] = jnp.concatenate([prev, x, nxt], axis=0).astype(BF16)

    tsub = ts // POOL_SUBTILES
    n_ext = tsub + 2 * POOL_HALO
    n_groups = len(POOL_WINDOWS)
    gw = win_ref.shape[1] // (2 * n_groups)

    def project(sub, g):
        r0 = sub * tsub
        r = jnp.dot(xe_ref[r0:r0 + n_ext], win_ref[:, 2 * g * gw:2 * (g + 1) * gw],
                    preferred_element_type=F32)
        return r[:, :gw], r[POOL_HALO:POOL_HALO + tsub, gw:]

    def mix(sub, g, u, gate):
        h = POOL_WINDOWS[g] // 2
        b = u + pltpu.roll(u, 1, 0)
        span = 2
        while span < h:
            b = b + pltpu.roll(b, span, 0)
            span *= 2
        if h == 1:
            wsum = b
        else:
            wsum = pltpu.roll(b, n_ext - (h - 1), 0) + pltpu.roll(b, 1, 0)
        t = j * ts + sub * tsub + lax.broadcasted_iota(jnp.int32, (tsub, 1), 0)
        cnt = (jnp.minimum(t + h, seq) - jnp.maximum(t - h, 0)).astype(F32)
        y = wsum[POOL_HALO:POOL_HALO + tsub] * (1.0 / cnt) - u[POOL_HALO:POOL_HALO + tsub]
        return (y * _silu(gate)).astype(BF16)

    units = [(sub, g) for sub in range(POOL_SUBTILES) for g in range(n_groups)]
    ahead = project(*units[0])
    acc = None
    for n, (sub, g) in enumerate(units):
        cur = ahead
        if n + 1 < len(units):
            ahead = project(*units[n + 1])
        a = mix(sub, g, *cur)
        w_g = wout_ref[g * gw:(g + 1) * gw, :]
        if g < n_groups - 1:
            part = jnp.dot(a, w_g, preferred_element_type=F32)
            acc = part if g == 0 else acc + part
            continue
        rc = tsub // POOL_TAIL_CHUNKS
        for k in range(POOL_TAIL_CHUNKS):
            lo = k * rc
            h = acc[lo:lo + rc] + jnp.dot(a[lo:lo + rc], w_g, preferred_element_type=F32)
            rows_ = slice(sub * tsub + lo, sub * tsub + lo + rc)
            z = DEEPNORM_ALPHA * x_ref[0, rows_, :] + h
            o_ref[0, rows_, :] = _layer_norm(z, g_ref[...], b_ref[...])


def _resident(shape):
    nd = len(shape)
    return pl.BlockSpec(shape, lambda *_: (0,) * nd, pipeline_mode=pl.Buffered(1))


def _nbytes(shape, dtype):
    return int(np.prod(shape)) * jnp.dtype(dtype).itemsize


def _vmem_limit(resident, pipelined, scratch, temporaries):
    return min(resident + 2 * pipelined + scratch + temporaries, V7X_VMEM_BYTES - V7X_VMEM_RESERVE_BYTES)


def _pool_layer(x, w_in, w_out, ln_g, ln_b, *, ts=1024):
    bsz, seq, d = x.shape
    assert seq % ts == 0 and ts % POOL_HALO == 0
    n_halo_blocks = seq // POOL_HALO
    per_tile = ts // POOL_HALO
    kernel = functools.partial(_pool_layer_kernel, seq=seq, ts=ts)
    tile_f32 = _nbytes((ts, d), F32)
    vmem_limit = _vmem_limit(
        resident=_nbytes(w_in.shape, w_in.dtype) + _nbytes(w_out.shape, w_out.dtype),
        pipelined=2 * tile_f32 + 2 * _nbytes((POOL_HALO, d), F32),
        scratch=_nbytes((ts + 2 * POOL_HALO, d), BF16),
        temporaries=6 * tile_f32)
    return pl.pallas_call(
        kernel,
        out_shape=jax.ShapeDtypeStruct(x.shape, F32),
        grid=(bsz, seq // ts),
        in_specs=[
            pl.BlockSpec((1, POOL_HALO, d), lambda b, j: (b, jnp.maximum(j * per_tile - 1, 0), 0)),
            pl.BlockSpec((1, ts, d), lambda b, j: (b, j, 0)),
            pl.BlockSpec((1, POOL_HALO, d), lambda b, j: (b, jnp.minimum((j + 1) * per_tile, n_halo_blocks - 1), 0)),
            _resident(w_in.shape), _resident(w_out.shape), _resident(ln_g.shape), _resident(ln_b.shape),
        ],
        out_specs=pl.BlockSpec((1, ts, d), lambda b, j: (b, j, 0)),
        scratch_shapes=[pltpu.VMEM((ts + 2 * POOL_HALO, d), BF16)],
        compiler_params=pltpu.CompilerParams(
            dimension_semantics=("arbitrary", "arbitrary"),
            vmem_limit_bytes=vmem_limit),
        name="pool_layer",
    )(x, x, x, w_in, w_out, ln_g, ln_b)


def _na_inproj_kernel(x_ref, w_ref, q_ref, k_ref, v_ref, gate_ref):
    xb = x_ref[0].astype(BF16)
    width = gate_ref.shape[-1]
    n_chunks = q_ref.shape[1]
    cw = q_ref.shape[-1]
    full = jnp.dot(xb, w_ref[...], preferred_element_type=F32)
    for part, ref in enumerate((q_ref, k_ref, v_ref)):
        r = full[:, part * width:(part + 1) * width]
        if part == 0:
            r = r * (NA_SCALE * LOG2_E)
        r = r.astype(BF16)
        for c in range(n_chunks):
            ref[0, c] = r[:, c * cw:(c + 1) * cw]
    gate_ref[0] = full[:, 3 * width:4 * width]


def _na_inproj(x, w_in, *, tm=512):
    bsz, seq, d = x.shape
    width = w_in.shape[1] // 4
    n_chunks = width // V7X_MXU_DIM
    chunked = jax.ShapeDtypeStruct((bsz, n_chunks, seq, V7X_MXU_DIM), BF16)
    chunk_spec = pl.BlockSpec((1, n_chunks, tm, V7X_MXU_DIM), lambda b, j: (b, 0, j, 0))
    vmem_limit = _vmem_limit(
        resident=_nbytes(w_in.shape, w_in.dtype),
        pipelined=_nbytes((tm, d), F32) + 3 * _nbytes((tm, width), BF16) + _nbytes((tm, width), F32),
        scratch=0,
        temporaries=2 * _nbytes((tm, 4 * width), F32))
    return pl.pallas_call(
        _na_inproj_kernel,
        out_shape=(chunked, chunked, chunked, jax.ShapeDtypeStruct((bsz, seq, width), F32)),
        grid=(bsz, seq // tm),
        in_specs=[pl.BlockSpec((1, tm, d), lambda b, j: (b, j, 0)), _resident(w_in.shape)],
        out_specs=(chunk_spec, chunk_spec, chunk_spec, pl.BlockSpec((1, tm, width), lambda b, j: (b, j, 0))),
        compiler_params=pltpu.CompilerParams(
            dimension_semantics=("arbitrary", "arbitrary"),
            vmem_limit_bytes=vmem_limit),
        name="na_inproj",
    )(x, w_in)


def _kv_window_start(blk, rows):
    return jnp.clip(blk * ROWS_PER_STEP - HALO_ROWS, 0, rows - (ROWS_PER_STEP + 2 * HALO_ROWS))


def _na_layer_kernel(q_ref, kbuf, vbuf, gate_ref, x_ref, tab_ref, wout_ref, g_ref, b_ref, o_ref, obuf, *, rows):
    blk = pl.program_id(1)
    n_chunks = q_ref.shape[1]
    cw = q_ref.shape[-1]
    window_row0 = _kv_window_start(blk, rows)

    def head_masks_for(n_rows):
        lane_head = lax.broadcasted_iota(jnp.int32, (n_rows, cw), 1) // HEAD_DIM
        return [lane_head == h for h in range(HEADS_PER_CHUNK)]

    head_masks = {nq: head_masks_for(nq) for nq in sorted({q1 - q0 for q0, q1, _ in QUERY_GROUPS})}
    n_blocks = WIN_H // KEY_ROWS_PER_BLOCK

    def row_geometry(i):
        r = blk * ROWS_PER_STEP + i
        rs = jnp.clip(r - HALO_ROWS, 0, rows - WIN_H)
        return i * GRID_W, (rs - window_row0) * GRID_W, r - rs

    def scores(geom, c, group):
        qoff, koff, row_off = geom
        q0, q1, k0 = group
        nq = q1 - q0
        qf = q_ref[0, c, qoff:qoff + GRID_W, :].astype(F32)
        starts = [pl.multiple_of(koff + kr * GRID_W + k0, KEY_COL_ALIGN) for kr in range(WIN_H)]
        kg = jnp.concatenate([kbuf[0, c, pl.ds(s0, KEY_COLS), :] for s0 in starts], axis=0)
        qs = jnp.concatenate([jnp.where(m, qf[q0:q1], 0.0) for m in head_masks[nq]],
                             axis=0).astype(BF16)
        s = lax.dot_general(qs, kg, (((1,), (1,)), ((), ())), preferred_element_type=F32)
        biased = []
        for h in range(HEADS_PER_CHUNK):
            bias = jnp.concatenate(
                [tab_ref[c * HEADS_PER_CHUNK + h, (WIN_H - 1) - row_off + KEY_ROWS_PER_BLOCK * m, q0:q1, :]
                 for m in range(n_blocks)], axis=1)
            biased.append(s[h * nq:(h + 1) * nq] + bias)
        s = jnp.concatenate(biased, axis=0)
        return (s, jnp.max(s, axis=-1, keepdims=True)), starts

    def finish(geom, c, group, s_and_max, starts):
        qoff = geom[0]
        q0, q1, _ = group
        nq = q1 - q0
        masks = head_masks[nq]
        s, s_max = s_and_max
        p = jnp.exp2(s - s_max)
        l = jnp.sum(p, axis=-1, keepdims=True)
        vg = jnp.concatenate([vbuf[0, c, pl.ds(s0, KEY_COLS), :] for s0 in starts], axis=0)
        pv = jnp.dot(p.astype(BF16), vg, preferred_element_type=F32)
        last = HEADS_PER_CHUNK - 1
        oc = pv[last * nq:]
        lc = jnp.broadcast_to(l[last * nq:], (nq, cw))
        for h in range(last - 1, -1, -1):
            oc = jnp.where(masks[h], pv[h * nq:(h + 1) * nq], oc)
            lc = jnp.where(masks[h], l[h * nq:(h + 1) * nq], lc)
        obuf[c, qoff + q0:qoff + q1, :] = oc * (1.0 / lc)

    def epilogue_pieces(row0, row1):
        tok = slice(row0 * GRID_W, row1 * GRID_W)
        state = {}

        def gate():
            o_part = jnp.concatenate([obuf[c, tok, :] for c in range(n_chunks)], axis=1)
            state["a"] = (o_part * _silu(gate_ref[0, tok, :])).astype(BF16)

        def project(nb):
            state[nb] = jnp.dot(state["a"], wout_ref[:, nb * cw:(nb + 1) * cw], preferred_element_type=F32)

        def normalise():
            h2 = jnp.concatenate([state[nb] for nb in range(n_chunks)], axis=1)
            z = DEEPNORM_ALPHA * x_ref[0, tok, :] + h2
            o_ref[0, tok, :] = _layer_norm(z, g_ref[...], b_ref[...])

        return [gate] + [functools.partial(project, nb) for nb in range(n_chunks)] + [normalise]

    units = [(i, c, g) for i in range(ROWS_PER_STEP) for c in range(n_chunks) for g in QUERY_GROUPS]
    geoms = [row_geometry(i) for i in range(ROWS_PER_STEP)]
    units_per_row = n_chunks * len(QUERY_GROUPS)
    part_ends = np.cumsum(EPILOGUE_ROW_SPLITS)
    assert part_ends[-1] == ROWS_PER_STEP
    pending, epilogue = [], []
    n_finished = 0
    for n_scored in range(len(units) + SCORE_LOOKAHEAD):
        if n_scored < len(units):
            i, c, g = units[n_scored]
            pending.append(((geoms[i], c, g), scores(geoms[i], c, g)))
        if n_scored >= SCORE_LOOKAHEAD:
            done, (s, starts) = pending.pop(0)
            finish(*done, s, starts)
            n_finished += 1
            if epilogue:
                epilogue.pop(0)()
            if n_finished % units_per_row == 0 and n_finished // units_per_row in part_ends:
                row1 = n_finished // units_per_row
                row0 = row1 - EPILOGUE_ROW_SPLITS[list(part_ends).index(row1)]
                epilogue += epilogue_pieces(row0, row1)
    for piece in epilogue:
        piece()


def _bias_table_kernel(y_ref, o_ref):
    q = lax.broadcasted_iota(jnp.int32, (GRID_W, V7X_LANES), 0)
    kcc = lax.broadcasted_iota(jnp.int32, (GRID_W, V7X_LANES), 1) % KEY_COLS
    cs = jnp.clip(q - WIN_W // 2, 0, GRID_W - WIN_W)
    k0 = jnp.zeros_like(q)
    for q0, _, first_key in QUERY_GROUPS:
        k0 = jnp.where(q >= q0, first_key, k0)
    kc = k0 + kcc
    valid = (kc >= cs) & (kc < cs + WIN_W)
    for d in range(o_ref.shape[1]):
        yb = jnp.broadcast_to(y_ref[0, d:d + 1, :], (GRID_W, V7X_LANES))
        t = jnp.zeros((GRID_W, V7X_LANES), F32)
        for q0, _, first_key in QUERY_GROUPS:
            shift = (-(first_key + WIN_W - 1)) % V7X_LANES
            rolled = pltpu.roll(yb, shift, 1, stride=1, stride_axis=0)
            t = jnp.where(q >= q0, rolled, t)
        o_ref[0, d] = jnp.where(valid, t * LOG2_E, MASK_VALUE)


def _bias_table(rpb):
    n_heads = rpb.shape[0]
    n_d = 2 * WIN_H - KEY_ROWS_PER_BLOCK
    padded = jnp.pad(rpb.astype(F32), ((0, 0), (0, 0), (0, KEY_COLS - rpb.shape[2])))
    y = jnp.stack([padded[:, d:d + KEY_ROWS_PER_BLOCK].reshape(n_heads, V7X_LANES) for d in range(n_d)], axis=1)
    return pl.pallas_call(
        _bias_table_kernel,
        out_shape=jax.ShapeDtypeStruct((n_heads, n_d, GRID_W, V7X_LANES), F32),
        grid=(n_heads,),
        in_specs=[pl.BlockSpec((1, n_d, V7X_LANES), lambda h: (h, 0, 0))],
        out_specs=pl.BlockSpec((1, n_d, GRID_W, V7X_LANES), lambda h: (h, 0, 0, 0)),
        compiler_params=pltpu.CompilerParams(dimension_semantics=("arbitrary",)),
        name="bias_table",
    )(y)


def _na_layer(q, k, v, gate, x, tab, w_out, ln_g, ln_b):
    bsz, seq, d = x.shape
    rows = seq // GRID_W
    n_chunks, cw = q.shape[1], q.shape[3]
    main = ROWS_PER_STEP * GRID_W
    window = (ROWS_PER_STEP + 2 * HALO_ROWS) * GRID_W
    assert rows % ROWS_PER_STEP == 0 and rows * GRID_W >= window
    main_spec = pl.BlockSpec((1, n_chunks, main, cw), lambda b, j: (b, 0, j, 0))
    kv_spec = pl.BlockSpec((pl.Element(1), pl.Element(n_chunks), pl.Element(window), pl.Element(cw)),
                           lambda b, j: (b, 0, _kv_window_start(j, rows) * GRID_W, 0))
    tok_spec = pl.BlockSpec((1, main, d), lambda b, j: (b, j, 0))
    kernel = functools.partial(_na_layer_kernel, rows=rows)
    tile_f32 = _nbytes((main, d), F32)
    vmem_limit = _vmem_limit(
        resident=_nbytes(tab.shape, tab.dtype) + _nbytes(w_out.shape, w_out.dtype),
        pipelined=_nbytes((main, d), BF16) + 2 * _nbytes((window, d), BF16) + 3 * tile_f32,
        scratch=tile_f32,
        temporaries=4 * tile_f32)
    return pl.pallas_call(
        kernel,
        out_shape=jax.ShapeDtypeStruct(x.shape, F32),
        grid=(bsz, rows // ROWS_PER_STEP),
        in_specs=[main_spec, kv_spec, kv_spec, tok_spec, tok_spec, _resident(tab.shape), _resident(w_out.shape),
                  _resident(ln_g.shape), _resident(ln_b.shape)],
        out_specs=tok_spec,
        scratch_shapes=[pltpu.VMEM((n_chunks, main, cw), F32)],
        compiler_params=pltpu.CompilerParams(
            dimension_semantics=("arbitrary", "arbitrary"),
            vmem_limit_bytes=vmem_limit),
        name="na_layer",
    )(q, k, v, gate, x, tab, w_out, ln_g, ln_b)


def _trunk(x, p):
    x1 = _pool_layer(x, p["w_in_pool"], p["w_out_pool"], p["ln_g0"], p["ln_b0"])
    q, k, v, gate = _na_inproj(x1, p["w_in_na"])
    return _na_layer(q, k, v, gate, x1, p["tab"], p["w_out_na"], p["ln_g1"], p["ln_b1"])


def kernel(x_prompt, x_sample, w_in_pool, w_grp_pool, scale_pool, w_out_pool, w_in_na, rpb_na, w_out_na, ln_g, ln_b):
    scale = scale_pool[0].reshape(1, -1).astype(F32)
    p = {
        "w_in_pool": _fold_pool_weights(w_in_pool[0], w_grp_pool[0], scale),
        "w_out_pool": w_out_pool[0].astype(BF16),
        "w_in_na": w_in_na[0].astype(BF16),
        "w_out_na": w_out_na[0].astype(BF16),
        "tab": _bias_table(rpb_na[0]),
        "ln_g0": ln_g[0].reshape(1, -1).astype(F32),
        "ln_b0": ln_b[0].reshape(1, -1).astype(F32),
        "ln_g1": ln_g[1].reshape(1, -1).astype(F32),
        "ln_b1": ln_b[1].reshape(1, -1).astype(F32),
    }
    return (_trunk(x_prompt, p), _trunk(x_sample, p))
```

```python
import functools

import jax
import jax.numpy as jnp
import numpy as np
from jax import lax
from jax.experimental import pallas as pl
from jax.experimental.pallas import tpu as pltpu

F32 = jnp.float32
BF16 = jnp.bfloat16

GRID_W = 64
POOL_WINDOWS = (2, 4, 8, 16)
HEAD_DIM = 32
WIN_H = 8
WIN_W = 16
NA_SCALE = HEAD_DIM ** -0.5
LOG2_E = 1.4426950408889634
LN_EPS = 1e-5
DEPTH = 2
DEEPNORM_ALPHA = (2 * DEPTH) ** 0.25

V7X_SUBLANES_F32 = 8
V7X_LANES = 128
V7X_MXU_DIM = 256
V7X_VMEM_BYTES = 64 * 1024 * 1024
V7X_VMEM_RESERVE_BYTES = 4 * 1024 * 1024

MASK_VALUE = -1e30
POOL_HALO = max(POOL_WINDOWS) // 2
HEADS_PER_CHUNK = V7X_MXU_DIM // HEAD_DIM
ROWS_PER_STEP = 8
HALO_ROWS = WIN_H // 2
V7X_SUBLANES_BF16 = 16
KEY_COLS = 2 * WIN_W
KEY_COL_ALIGN = V7X_SUBLANES_BF16
KEY_ROWS_PER_BLOCK = V7X_LANES // KEY_COLS


def _query_groups():
    groups = []
    for q in range(GRID_W):
        cs = min(max(q - WIN_W // 2, 0), GRID_W - WIN_W)
        k0 = min(cs // KEY_COL_ALIGN * KEY_COL_ALIGN, GRID_W - KEY_COLS)
        assert k0 <= cs and cs + WIN_W <= k0 + KEY_COLS
        if groups and groups[-1][2] == k0:
            groups[-1][1] = q + 1
        else:
            groups.append([q, q + 1, k0])
    assert all(g[0] % V7X_SUBLANES_F32 == 0 for g in groups)
    return tuple(tuple(g) for g in groups)


QUERY_GROUPS = _query_groups()
SCORE_LOOKAHEAD = 4
POOL_TAIL_CHUNKS = 4
POOL_SUBTILES = 1
EPILOGUE_ROW_SPLITS = (6, 2)


def _silu(x):
    return x / (1.0 + jnp.exp(-x))


def _layer_norm(z, g, b):
    mu = jnp.mean(z, axis=-1, keepdims=True)
    zc = z - mu
    var = jnp.mean(zc * zc, axis=-1, keepdims=True)
    return zc * lax.rsqrt(var + LN_EPS) * g + b


def _fold_pool_weights_kernel(wu_ref, wgate_ref, wgrp_ref, scale_ref, o_ref):
    gw = wu_ref.shape[1]
    folded = jnp.dot(wu_ref[...], wgrp_ref[0], preferred_element_type=F32, precision=lax.Precision.HIGHEST)
    o_ref[:, :gw] = (folded * scale_ref[...]).astype(o_ref.dtype)
    o_ref[:, gw:] = wgate_ref[...].astype(o_ref.dtype)


def _fold_pool_weights(w_in, w_grp, scale):
    d = w_in.shape[0]
    n_groups, gw, _ = w_grp.shape
    return pl.pallas_call(
        _fold_pool_weights_kernel,
        out_shape=jax.ShapeDtypeStruct((d, 2 * n_groups * gw), BF16),
        grid=(n_groups,),
        in_specs=[pl.BlockSpec((d, gw), lambda g: (0, g)),
                  pl.BlockSpec((d, gw), lambda g: (0, n_groups + g)),
                  pl.BlockSpec((1, gw, gw), lambda g: (g, 0, 0)),
                  pl.BlockSpec((1, gw), lambda g: (0, g))],
        out_specs=pl.BlockSpec((d, 2 * gw), lambda g: (0, g)),
        compiler_params=pltpu.CompilerParams(dimension_semantics=("arbitrary",)),
        name="fold_pool_weights",
    )(w_in, w_in, w_grp, scale)


def _pool_layer_kernel(xp_ref, x_ref, xn_ref, win_ref, wout_ref, g_ref, b_ref, o_ref, xe_ref, *, seq, ts):
    j = pl.program_id(1)
    nj = pl.num_programs(1)
    x = x_ref[0]
    prev = jnp.where(j > 0, xp_ref[0], 0.0)
    nxt = jnp.where(j < nj - 1, xn_ref[0], 0.0)
    xe_ref[...] = jnp.concatenate([prev, x, nxt], axis=0).astype(BF16)

    tsub = ts // POOL_SUBTILES
    n_ext = tsub + 2 * POOL_HALO
    n_groups = len(POOL_WINDOWS)
    gw = win_ref.shape[1] // (2 * n_groups)

    def project(sub, g):
        r0 = sub * tsub
        r = jnp.dot(xe_ref[r0:r0 + n_ext], win_ref[:, 2 * g * gw:2 * (g + 1) * gw],
                    preferred_element_type=F32)
        return r[:, :gw], r[POOL_HALO:POOL_HALO + tsub, gw:]

    def mix(sub, g, u, gate):
        h = POOL_WINDOWS[g] // 2
        if h == 1:
            wsum = u + pltpu.roll(u, 1, 0)
        else:
            fwd = u + pltpu.roll(u, n_ext - 1, 0)
            span = 2
            while span < h:
                fwd = fwd + pltpu.roll(fwd, n_ext - span, 0)
                span *= 2
            wsum = pltpu.roll(fwd, h, 0) + fwd
        t = j * ts + sub * tsub + lax.broadcasted_iota(jnp.int32, (tsub, 1), 0)
        cnt = (jnp.minimum(t + h, seq) - jnp.maximum(t - h, 0)).astype(F32)
        y = wsum[POOL_HALO:POOL_HALO + tsub] * (1.0 / cnt) - u[POOL_HALO:POOL_HALO + tsub]
        return (y * _silu(gate)).astype(BF16)

    units = [(sub, g) for sub in range(POOL_SUBTILES) for g in range(n_groups)]
    ahead = project(*units[0])
    acc = None
    for n, (sub, g) in enumerate(units):
        cur = ahead
        if n + 1 < len(units):
            ahead = project(*units[n + 1])
        a = mix(sub, g, *cur)
        w_g = wout_ref[g * gw:(g + 1) * gw, :]
        if g < n_groups - 1:
            part = jnp.dot(a, w_g, preferred_element_type=F32)
            acc = part if g == 0 else acc + part
            continue
        rc = tsub // POOL_TAIL_CHUNKS
        for k in range(POOL_TAIL_CHUNKS):
            lo = k * rc
            h = acc[lo:lo + rc] + jnp.dot(a[lo:lo + rc], w_g, preferred_element_type=F32)
            rows_ = slice(sub * tsub + lo, sub * tsub + lo + rc)
            z = DEEPNORM_ALPHA * x_ref[0, rows_, :] + h
            o_ref[0, rows_, :] = _layer_norm(z, g_ref[...], b_ref[...])


def _resident(shape):
    nd = len(shape)
    return pl.BlockSpec(shape, lambda *_: (0,) * nd, pipeline_mode=pl.Buffered(1))


def _nbytes(shape, dtype):
    return int(np.prod(shape)) * jnp.dtype(dtype).itemsize


def _vmem_limit(resident, pipelined, scratch, temporaries):
    return min(resident + 2 * pipelined + scratch + temporaries, V7X_VMEM_BYTES - V7X_VMEM_RESERVE_BYTES)


def _pool_layer(x, w_in, w_out, ln_g, ln_b, *, ts=1024):
    bsz, seq, d = x.shape
    assert seq % ts == 0 and ts % POOL_HALO == 0
    n_halo_blocks = seq // POOL_HALO
    per_tile = ts // POOL_HALO
    kernel = functools.partial(_pool_layer_kernel, seq=seq, ts=ts)
    tile_f32 = _nbytes((ts, d), F32)
    vmem_limit = _vmem_limit(
        resident=_nbytes(w_in.shape, w_in.dtype) + _nbytes(w_out.shape, w_out.dtype),
        pipelined=2 * tile_f32 + 2 * _nbytes((POOL_HALO, d), F32),
        scratch=_nbytes((ts + 2 * POOL_HALO, d), BF16),
        temporaries=6 * tile_f32)
    return pl.pallas_call(
        kernel,
        out_shape=jax.ShapeDtypeStruct(x.shape, F32),
        grid=(bsz, seq // ts),
        in_specs=[
            pl.BlockSpec((1, POOL_HALO, d), lambda b, j: (b, jnp.maximum(j * per_tile - 1, 0), 0)),
            pl.BlockSpec((1, ts, d), lambda b, j: (b, j, 0)),
            pl.BlockSpec((1, POOL_HALO, d), lambda b, j: (b, jnp.minimum((j + 1) * per_tile, n_halo_blocks - 1), 0)),
            _resident(w_in.shape), _resident(w_out.shape), _resident(ln_g.shape), _resident(ln_b.shape),
        ],
        out_specs=pl.BlockSpec((1, ts, d), lambda b, j: (b, j, 0)),
        scratch_shapes=[pltpu.VMEM((ts + 2 * POOL_HALO, d), BF16)],
        compiler_params=pltpu.CompilerParams(
            dimension_semantics=("arbitrary", "arbitrary"),
            vmem_limit_bytes=vmem_limit),
        name="pool_layer",
    )(x, x, x, w_in, w_out, ln_g, ln_b)


def _na_inproj_kernel(x_ref, w_ref, q_ref, k_ref, v_ref, gate_ref):
    xb = x_ref[0].astype(BF16)
    width = gate_ref.shape[-1]
    n_chunks = q_ref.shape[1]
    cw = q_ref.shape[-1]
    full = jnp.dot(xb, w_ref[...], preferred_element_type=F32)
    for part, ref in enumerate((q_ref, k_ref, v_ref)):
        r = full[:, part * width:(part + 1) * width]
        if part == 0:
            r = r * (NA_SCALE * LOG2_E)
        r = r.astype(BF16)
        for c in range(n_chunks):
            ref[0, c] = r[:, c * cw:(c + 1) * cw]
    gate_ref[0] = full[:, 3 * width:4 * width]


def _na_inproj(x, w_in, *, tm=512):
    bsz, seq, d = x.shape
    width = w_in.shape[1] // 4
    n_chunks = width // V7X_MXU_DIM
    chunked = jax.ShapeDtypeStruct((bsz, n_chunks, seq, V7X_MXU_DIM), BF16)
    chunk_spec = pl.BlockSpec((1, n_chunks, tm, V7X_MXU_DIM), lambda b, j: (b, 0, j, 0))
    vmem_limit = _vmem_limit(
        resident=_nbytes(w_in.shape, w_in.dtype),
        pipelined=_nbytes((tm, d), F32) + 3 * _nbytes((tm, width), BF16) + _nbytes((tm, width), F32),
        scratch=0,
        temporaries=2 * _nbytes((tm, 4 * width), F32))
    return pl.pallas_call(
        _na_inproj_kernel,
        out_shape=(chunked, chunked, chunked, jax.ShapeDtypeStruct((bsz, seq, width), F32)),
        grid=(bsz, seq // tm),
        in_specs=[pl.BlockSpec((1, tm, d), lambda b, j: (b, j, 0)), _resident(w_in.shape)],
        out_specs=(chunk_spec, chunk_spec, chunk_spec, pl.BlockSpec((1, tm, width), lambda b, j: (b, j, 0))),
        compiler_params=pltpu.CompilerParams(
            dimension_semantics=("arbitrary", "arbitrary"),
            vmem_limit_bytes=vmem_limit),
        name="na_inproj",
    )(x, w_in)


def _kv_window_start(blk, rows):
    return jnp.clip(blk * ROWS_PER_STEP - HALO_ROWS, 0, rows - (ROWS_PER_STEP + 2 * HALO_ROWS))


def _na_layer_kernel(q_ref, kbuf, vbuf, gate_ref, x_ref, tab_ref, wout_ref, g_ref, b_ref, o_ref, obuf, *, rows):
    blk = pl.program_id(1)
    n_chunks = q_ref.shape[1]
    cw = q_ref.shape[-1]
    window_row0 = _kv_window_start(blk, rows)

    def head_masks_for(n_rows):
        lane_head = lax.broadcasted_iota(jnp.int32, (n_rows, cw), 1) // HEAD_DIM
        return [lane_head == h for h in range(HEADS_PER_CHUNK)]

    head_masks = {nq: head_masks_for(nq) for nq in sorted({q1 - q0 for q0, q1, _ in QUERY_GROUPS})}
    n_blocks = WIN_H // KEY_ROWS_PER_BLOCK

    def row_geometry(i):
        r = blk * ROWS_PER_STEP + i
        rs = jnp.clip(r - HALO_ROWS, 0, rows - WIN_H)
        return i * GRID_W, (rs - window_row0) * GRID_W, r - rs

    def scores(geom, c, group):
        qoff, koff, row_off = geom
        q0, q1, k0 = group
        nq = q1 - q0
        qf = q_ref[0, c, qoff:qoff + GRID_W, :].astype(F32)
        starts = [pl.multiple_of(koff + kr * GRID_W + k0, KEY_COL_ALIGN) for kr in range(WIN_H)]
        kg = jnp.concatenate([kbuf[0, c, pl.ds(s0, KEY_COLS), :] for s0 in starts], axis=0)
        qs = jnp.concatenate([jnp.where(m, qf[q0:q1], 0.0) for m in head_masks[nq]],
                             axis=0).astype(BF16)
        s = lax.dot_general(qs, kg, (((1,), (1,)), ((), ())), preferred_element_type=F32)
        biased = []
        for h in range(HEADS_PER_CHUNK):
            bias = jnp.concatenate(
                [tab_ref[c * HEADS_PER_CHUNK + h, (WIN_H - 1) - row_off + KEY_ROWS_PER_BLOCK * m, q0:q1, :]
                 for m in range(n_blocks)], axis=1)
            biased.append(s[h * nq:(h + 1) * nq] + bias)
        s = jnp.concatenate(biased, axis=0)
        return (s, jnp.max(s, axis=-1, keepdims=True)), starts

    def finish(geom, c, group, s_and_max, starts):
        qoff = geom[0]
        q0, q1, _ = group
        nq = q1 - q0
        masks = head_masks[nq]
        s, s_max = s_and_max
        p = jnp.exp2(s - s_max)
        l = jnp.sum(p, axis=-1, keepdims=True)
        vg = jnp.concatenate([vbuf[0, c, pl.ds(s0, KEY_COLS), :] for s0 in starts], axis=0)
        pv = jnp.dot(p.astype(BF16), vg, preferred_element_type=F32)
        last = HEADS_PER_CHUNK - 1
        oc = pv[last * nq:]
        lc = jnp.broadcast_to(l[last * nq:], (nq, cw))
        for h in range(last - 1, -1, -1):
            oc = jnp.where(masks[h], pv[h * nq:(h + 1) * nq], oc)
            lc = jnp.where(masks[h], l[h * nq:(h + 1) * nq], lc)
        obuf[c, qoff + q0:qoff + q1, :] = oc * (1.0 / lc)

    def epilogue_pieces(row0, row1):
        tok = slice(row0 * GRID_W, row1 * GRID_W)
        state = {}

        def gate():
            o_part = jnp.concatenate([obuf[c, tok, :] for c in range(n_chunks)], axis=1)
            state["a"] = (o_part * _silu(gate_ref[0, tok, :])).astype(BF16)

        def project(nb):
            state[nb] = jnp.dot(state["a"], wout_ref[:, nb * cw:(nb + 1) * cw], preferred_element_type=F32)

        def normalise():
            h2 = jnp.concatenate([state[nb] for nb in range(n_chunks)], axis=1)
            z = DEEPNORM_ALPHA * x_ref[0, tok, :] + h2
            o_ref[0, tok, :] = _layer_norm(z, g_ref[...], b_ref[...])

        return [gate] + [functools.partial(project, nb) for nb in range(n_chunks)] + [normalise]

    units = [(i, c, g) for i in range(ROWS_PER_STEP) for c in range(n_chunks) for g in QUERY_GROUPS]
    geoms = [row_geometry(i) for i in range(ROWS_PER_STEP)]
    units_per_row = n_chunks * len(QUERY_GROUPS)
    part_ends = np.cumsum(EPILOGUE_ROW_SPLITS)
    assert part_ends[-1] == ROWS_PER_STEP
    pending, epilogue = [], []
    n_finished = 0
    for n_scored in range(len(units) + SCORE_LOOKAHEAD):
        if n_scored < len(units):
            i, c, g = units[n_scored]
            pending.append(((geoms[i], c, g), scores(geoms[i], c, g)))
        if n_scored >= SCORE_LOOKAHEAD:
            done, (s, starts) = pending.pop(0)
            finish(*done, s, starts)
            n_finished += 1
            if epilogue:
                epilogue.pop(0)()
            if n_finished % units_per_row == 0 and n_finished // units_per_row in part_ends:
                row1 = n_finished // units_per_row
                row0 = row1 - EPILOGUE_ROW_SPLITS[list(part_ends).index(row1)]
                epilogue += epilogue_pieces(row0, row1)
    for piece in epilogue:
        piece()


def _bias_table_kernel(y_ref, o_ref):
    q = lax.broadcasted_iota(jnp.int32, (GRID_W, V7X_LANES), 0)
    kcc = lax.broadcasted_iota(jnp.int32, (GRID_W, V7X_LANES), 1) % KEY_COLS
    cs = jnp.clip(q - WIN_W // 2, 0, GRID_W - WIN_W)
    k0 = jnp.zeros_like(q)
    for q0, _, first_key in QUERY_GROUPS:
        k0 = jnp.where(q >= q0, first_key, k0)
    kc = k0 + kcc
    valid = (kc >= cs) & (kc < cs + WIN_W)
    for d in range(o_ref.shape[1]):
        yb = jnp.broadcast_to(y_ref[0, d:d + 1, :], (GRID_W, V7X_LANES))
        t = jnp.zeros((GRID_W, V7X_LANES), F32)
        for q0, _, first_key in QUERY_GROUPS:
            shift = (-(first_key + WIN_W - 1)) % V7X_LANES
            rolled = pltpu.roll(yb, shift, 1, stride=1, stride_axis=0)
            t = jnp.where(q >= q0, rolled, t)
        o_ref[0, d] = jnp.where(valid, t * LOG2_E, MASK_VALUE)


def _bias_table(rpb):
    n_heads = rpb.shape[0]
    n_d = 2 * WIN_H - KEY_ROWS_PER_BLOCK
    padded = jnp.pad(rpb.astype(F32), ((0, 0), (0, 0), (0, KEY_COLS - rpb.shape[2])))
    y = jnp.stack([padded[:, d:d + KEY_ROWS_PER_BLOCK].reshape(n_heads, V7X_LANES) for d in range(n_d)], axis=1)
    return pl.pallas_call(
        _bias_table_kernel,
        out_shape=jax.ShapeDtypeStruct((n_heads, n_d, GRID_W, V7X_LANES), F32),
        grid=(n_heads,),
        in_specs=[pl.BlockSpec((1, n_d, V7X_LANES), lambda h: (h, 0, 0))],
        out_specs=pl.BlockSpec((1, n_d, GRID_W, V7X_LANES), lambda h: (h, 0, 0, 0)),
        compiler_params=pltpu.CompilerParams(dimension_semantics=("arbitrary",)),
        name="bias_table",
    )(y)


def _na_layer(q, k, v, gate, x, tab, w_out, ln_g, ln_b):
    bsz, seq, d = x.shape
    rows = seq // GRID_W
    n_chunks, cw = q.shape[1], q.shape[3]
    main = ROWS_PER_STEP * GRID_W
    window = (ROWS_PER_STEP + 2 * HALO_ROWS) * GRID_W
    assert rows % ROWS_PER_STEP == 0 and rows * GRID_W >= window
    main_spec = pl.BlockSpec((1, n_chunks, main, cw), lambda b, j: (b, 0, j, 0))
    kv_spec = pl.BlockSpec((pl.Element(1), pl.Element(n_chunks), pl.Element(window), pl.Element(cw)),
                           lambda b, j: (b, 0, _kv_window_start(j, rows) * GRID_W, 0))
    tok_spec = pl.BlockSpec((1, main, d), lambda b, j: (b, j, 0))
    kernel = functools.partial(_na_layer_kernel, rows=rows)
    tile_f32 = _nbytes((main, d), F32)
    vmem_limit = _vmem_limit(
        resident=_nbytes(tab.shape, tab.dtype) + _nbytes(w_out.shape, w_out.dtype),
        pipelined=_nbytes((main, d), BF16) + 2 * _nbytes((window, d), BF16) + 3 * tile_f32,
        scratch=tile_f32,
        temporaries=4 * tile_f32)
    return pl.pallas_call(
        kernel,
        out_shape=jax.ShapeDtypeStruct(x.shape, F32),
        grid=(bsz, rows // ROWS_PER_STEP),
        in_specs=[main_spec, kv_spec, kv_spec, tok_spec, tok_spec, _resident(tab.shape), _resident(w_out.shape),
                  _resident(ln_g.shape), _resident(ln_b.shape)],
        out_specs=tok_spec,
        scratch_shapes=[pltpu.VMEM((n_chunks, main, cw), F32)],
        compiler_params=pltpu.CompilerParams(
            dimension_semantics=("arbitrary", "arbitrary"),
            vmem_limit_bytes=vmem_limit),
        name="na_layer",
    )(q, k, v, gate, x, tab, w_out, ln_g, ln_b)


def _trunk(x, p):
    x1 = _pool_layer(x, p["w_in_pool"], p["w_out_pool"], p["ln_g0"], p["ln_b0"])
    q, k, v, gate = _na_inproj(x1, p["w_in_na"])
    return _na_layer(q, k, v, gate, x1, p["tab"], p["w_out_na"], p["ln_g1"], p["ln_b1"])


def kernel(x_prompt, x_sample, w_in_pool, w_grp_pool, scale_pool, w_out_pool, w_in_na, rpb_na, w_out_na, ln_g, ln_b):
    scale = scale_pool[0].reshape(1, -1).astype(F32)
    p = {
        "w_in_pool": _fold_pool_weights(w_in_pool[0], w_grp_pool[0], scale),
        "w_out_pool": w_out_pool[0].astype(BF16),
        "w_in_na": w_in_na[0].astype(BF16),
        "w_out_na": w_out_na[0].astype(BF16),
        "tab": _bias_table(rpb_na[0]),
        "ln_g0": ln_g[0].reshape(1, -1).astype(F32),
        "ln_b0": ln_b[0].reshape(1, -1).astype(F32),
        "ln_g1": ln_g[1].reshape(1, -1).astype(F32),
        "ln_b1": ln_b[1].reshape(1, -1).astype(F32),
    }
    return (_trunk(x_prompt, p), _trunk(x_sample, p))
```

```python
import functools

import jax
import jax.numpy as jnp
import numpy as np
from jax import lax
from jax.experimental import pallas as pl
from jax.experimental.pallas import tpu as pltpu

F32 = jnp.float32
BF16 = jnp.bfloat16

GRID_W = 64
POOL_WINDOWS = (2, 4, 8, 16)
HEAD_DIM = 32
WIN_H = 8
WIN_W = 16
NA_SCALE = HEAD_DIM ** -0.5
LOG2_E = 1.4426950408889634
LN_EPS = 1e-5
DEPTH = 2
DEEPNORM_ALPHA = (2 * DEPTH) ** 0.25

V7X_SUBLANES_F32 = 8
V7X_LANES = 128
V7X_MXU_DIM = 256
V7X_VMEM_BYTES = 64 * 1024 * 1024
V7X_VMEM_RESERVE_BYTES = 4 * 1024 * 1024

MASK_VALUE = -1e30
POOL_HALO = max(POOL_WINDOWS) // 2
HEADS_PER_CHUNK = V7X_MXU_DIM // HEAD_DIM
ROWS_PER_STEP = 8
HALO_ROWS = WIN_H // 2
V7X_SUBLANES_BF16 = 16
KEY_COLS = 2 * WIN_W
KEY_COL_ALIGN = V7X_SUBLANES_BF16
KEY_ROWS_PER_BLOCK = V7X_LANES // KEY_COLS


def _query_groups():
    groups = []
    for q in range(GRID_W):
        cs = min(max(q - WIN_W // 2, 0), GRID_W - WIN_W)
        k0 = min(cs // KEY_COL_ALIGN * KEY_COL_ALIGN, GRID_W - KEY_COLS)
        assert k0 <= cs and cs + WIN_W <= k0 + KEY_COLS
        if groups and groups[-1][2] == k0:
            groups[-1][1] = q + 1
        else:
            groups.append([q, q + 1, k0])
    assert all(g[0] % V7X_SUBLANES_F32 == 0 for g in groups)
    return tuple(tuple(g) for g in groups)


QUERY_GROUPS = _query_groups()
SCORE_LOOKAHEAD = 4
POOL_TAIL_CHUNKS = 4
POOL_SUBTILES = 1
EPILOGUE_ROW_SPLITS = (6, 2)


def _silu(x):
    return x / (1.0 + jnp.exp(-x))


def _layer_norm(z, g, b):
    mu = jnp.mean(z, axis=-1, keepdims=True)
    zc = z - mu
    var = jnp.mean(zc * zc, axis=-1, keepdims=True)
    return zc * lax.rsqrt(var + LN_EPS) * g + b


def _fold_pool_weights_kernel(wu_ref, wgate_ref, wgrp_ref, scale_ref, o_ref):
    gw = wu_ref.shape[1]
    folded = jnp.dot(wu_ref[...], wgrp_ref[0], preferred_element_type=F32, precision=lax.Precision.HIGHEST)
    o_ref[:, :gw] = (folded * scale_ref[...]).astype(o_ref.dtype)
    o_ref[:, gw:] = wgate_ref[...].astype(o_ref.dtype)


def _fold_pool_weights(w_in, w_grp, scale):
    d = w_in.shape[0]
    n_groups, gw, _ = w_grp.shape
    return pl.pallas_call(
        _fold_pool_weights_kernel,
        out_shape=jax.ShapeDtypeStruct((d, 2 * n_groups * gw), BF16),
        grid=(n_groups,),
        in_specs=[pl.BlockSpec((d, gw), lambda g: (0, g)),
                  pl.BlockSpec((d, gw), lambda g: (0, n_groups + g)),
                  pl.BlockSpec((1, gw, gw), lambda g: (g, 0, 0)),
                  pl.BlockSpec((1, gw), lambda g: (0, g))],
        out_specs=pl.BlockSpec((d, 2 * gw), lambda g: (0, g)),
        compiler_params=pltpu.CompilerParams(dimension_semantics=("arbitrary",)),
        name="fold_pool_weights",
    )(w_in, w_in, w_grp, scale)


def _pool_layer_kernel(xp_ref, x_ref, xn_ref, win_ref, wout_ref, g_ref, b_ref, o_ref, xe_ref, *, seq, ts):
    j = pl.program_id(1)
    nj = pl.num_programs(1)
    x = x_ref[0]
    prev = jnp.where(j > 0, xp_ref[0], 0.0)
    nxt = jnp.where(j < nj - 1, xn_ref[0], 0.0)
    xe_ref[...] = jnp.concatenate([prev, x, nxt], axis=0).astype(BF16)

    tsub = ts // POOL_SUBTILES
    n_ext = tsub + 2 * POOL_HALO
    n_groups = len(POOL_WINDOWS)
    gw = win_ref.shape[1] // (2 * n_groups)

    def project(sub, g):
        r0 = sub * tsub
        r = jnp.dot(xe_ref[r0:r0 + n_ext], win_ref[:, 2 * g * gw:2 * (g + 1) * gw],
                    preferred_element_type=F32)
        return r[:, :gw], r[POOL_HALO:POOL_HALO + tsub, gw:]

    def mix(sub, g, u, gate):
        h = POOL_WINDOWS[g] // 2
        if h == 1:
            wsum = u + pltpu.roll(u, 1, 0)
        else:
            fwd = u + pltpu.roll(u, n_ext - 1, 0)
            span = 2
            while span < h:
                fwd = fwd + pltpu.roll(fwd, n_ext - span, 0)
                span *= 2
            wsum = pltpu.roll(fwd, h, 0) + fwd
        t = j * ts + sub * tsub + lax.broadcasted_iota(jnp.int32, (tsub, 1), 0)
        cnt = (jnp.minimum(t + h, seq) - jnp.maximum(t - h, 0)).astype(F32)
        y = wsum[POOL_HALO:POOL_HALO + tsub] * (1.0 / cnt) - u[POOL_HALO:POOL_HALO + tsub]
        return (y * _silu(gate)).astype(BF16)

    units = [(sub, g) for sub in range(POOL_SUBTILES) for g in range(n_groups)]
    ahead = project(*units[0])
    acc = None
    assert n_groups % 2 == 0
    for n, (sub, g) in enumerate(units):
        cur = ahead
        if n + 1 < len(units):
            ahead = project(*units[n + 1])
        a = mix(sub, g, *cur)
        if g % 2 == 0:
            a_even = a
            continue
        a = jnp.concatenate([a_even, a], axis=1)
        w_g = wout_ref[(g - 1) * gw:(g + 1) * gw, :]
        if g < n_groups - 1:
            part = jnp.dot(a, w_g, preferred_element_type=F32)
            acc = part if g == 1 else acc + part
            continue
        rc = tsub // POOL_TAIL_CHUNKS
        for k in range(POOL_TAIL_CHUNKS):
            lo = k * rc
            h = acc[lo:lo + rc] + jnp.dot(a[lo:lo + rc], w_g, preferred_element_type=F32)
            rows_ = slice(sub * tsub + lo, sub * tsub + lo + rc)
            z = DEEPNORM_ALPHA * x_ref[0, rows_, :] + h
            o_ref[0, rows_, :] = _layer_norm(z, g_ref[...], b_ref[...])


def _resident(shape):
    nd = len(shape)
    return pl.BlockSpec(shape, lambda *_: (0,) * nd, pipeline_mode=pl.Buffered(1))


def _nbytes(shape, dtype):
    return int(np.prod(shape)) * jnp.dtype(dtype).itemsize


def _vmem_limit(resident, pipelined, scratch, temporaries):
    return min(resident + 2 * pipelined + scratch + temporaries, V7X_VMEM_BYTES - V7X_VMEM_RESERVE_BYTES)


def _pool_layer(x, w_in, w_out, ln_g, ln_b, *, ts=1024):
    bsz, seq, d = x.shape
    assert seq % ts == 0 and ts % POOL_HALO == 0
    n_halo_blocks = seq // POOL_HALO
    per_tile = ts // POOL_HALO
    kernel = functools.partial(_pool_layer_kernel, seq=seq, ts=ts)
    tile_f32 = _nbytes((ts, d), F32)
    vmem_limit = _vmem_limit(
        resident=_nbytes(w_in.shape, w_in.dtype) + _nbytes(w_out.shape, w_out.dtype),
        pipelined=2 * tile_f32 + 2 * _nbytes((POOL_HALO, d), F32),
        scratch=_nbytes((ts + 2 * POOL_HALO, d), BF16),
        temporaries=6 * tile_f32)
    return pl.pallas_call(
        kernel,
        out_shape=jax.ShapeDtypeStruct(x.shape, F32),
        grid=(bsz, seq // ts),
        in_specs=[
            pl.BlockSpec((1, POOL_HALO, d), lambda b, j: (b, jnp.maximum(j * per_tile - 1, 0), 0)),
            pl.BlockSpec((1, ts, d), lambda b, j: (b, j, 0)),
            pl.BlockSpec((1, POOL_HALO, d), lambda b, j: (b, jnp.minimum((j + 1) * per_tile, n_halo_blocks - 1), 0)),
            _resident(w_in.shape), _resident(w_out.shape), _resident(ln_g.shape), _resident(ln_b.shape),
        ],
        out_specs=pl.BlockSpec((1, ts, d), lambda b, j: (b, j, 0)),
        scratch_shapes=[pltpu.VMEM((ts + 2 * POOL_HALO, d), BF16)],
        compiler_params=pltpu.CompilerParams(
            dimension_semantics=("arbitrary", "arbitrary"),
            vmem_limit_bytes=vmem_limit),
        name="pool_layer",
    )(x, x, x, w_in, w_out, ln_g, ln_b)


def _na_inproj_kernel(x_ref, w_ref, q_ref, k_ref, v_ref, gate_ref):
    xb = x_ref[0].astype(BF16)
    width = gate_ref.shape[-1]
    n_chunks = q_ref.shape[1]
    cw = q_ref.shape[-1]
    full = jnp.dot(xb, w_ref[...], preferred_element_type=F32)
    for part, ref in enumerate((q_ref, k_ref, v_ref)):
        r = full[:, part * width:(part + 1) * width]
        if part == 0:
            r = r * (NA_SCALE * LOG2_E)
        r = r.astype(BF16)
        for c in range(n_chunks):
            ref[0, c] = r[:, c * cw:(c + 1) * cw]
    gate_ref[0] = full[:, 3 * width:4 * width]


def _na_inproj(x, w_in, *, tm=512):
    bsz, seq, d = x.shape
    width = w_in.shape[1] // 4
    n_chunks = width // V7X_MXU_DIM
    chunked = jax.ShapeDtypeStruct((bsz, n_chunks, seq, V7X_MXU_DIM), BF16)
    chunk_spec = pl.BlockSpec((1, n_chunks, tm, V7X_MXU_DIM), lambda b, j: (b, 0, j, 0))
    vmem_limit = _vmem_limit(
        resident=_nbytes(w_in.shape, w_in.dtype),
        pipelined=_nbytes((tm, d), F32) + 3 * _nbytes((tm, width), BF16) + _nbytes((tm, width), F32),
        scratch=0,
        temporaries=2 * _nbytes((tm, 4 * width), F32))
    return pl.pallas_call(
        _na_inproj_kernel,
        out_shape=(chunked, chunked, chunked, jax.ShapeDtypeStruct((bsz, seq, width), F32)),
        grid=(bsz, seq // tm),
        in_specs=[pl.BlockSpec((1, tm, d), lambda b, j: (b, j, 0)), _resident(w_in.shape)],
        out_specs=(chunk_spec, chunk_spec, chunk_spec, pl.BlockSpec((1, tm, width), lambda b, j: (b, j, 0))),
        compiler_params=pltpu.CompilerParams(
            dimension_semantics=("arbitrary", "arbitrary"),
            vmem_limit_bytes=vmem_limit),
        name="na_inproj",
    )(x, w_in)


def _kv_window_start(blk, rows):
    return jnp.clip(blk * ROWS_PER_STEP - HALO_ROWS, 0, rows - (ROWS_PER_STEP + 2 * HALO_ROWS))


def _na_layer_kernel(q_ref, kbuf, vbuf, gate_ref, x_ref, tab_ref, wout_ref, g_ref, b_ref, o_ref, obuf, *, rows):
    blk = pl.program_id(1)
    n_chunks = q_ref.shape[1]
    cw = q_ref.shape[-1]
    window_row0 = _kv_window_start(blk, rows)

    def head_masks_for(n_rows):
        lane_head = lax.broadcasted_iota(jnp.int32, (n_rows, cw), 1) // HEAD_DIM
        return [lane_head == h for h in range(HEADS_PER_CHUNK)]

    head_masks = {nq: head_masks_for(nq) for nq in sorted({q1 - q0 for q0, q1, _ in QUERY_GROUPS})}
    n_blocks = WIN_H // KEY_ROWS_PER_BLOCK

    def row_geometry(i):
        r = blk * ROWS_PER_STEP + i
        rs = jnp.clip(r - HALO_ROWS, 0, rows - WIN_H)
        return i * GRID_W, (rs - window_row0) * GRID_W, r - rs

    def scores(geom, c, group):
        qoff, koff, row_off = geom
        q0, q1, k0 = group
        nq = q1 - q0
        qf = q_ref[0, c, qoff:qoff + GRID_W, :].astype(F32)
        starts = [pl.multiple_of(koff + kr * GRID_W + k0, KEY_COL_ALIGN) for kr in range(WIN_H)]
        kg = jnp.concatenate([kbuf[0, c, pl.ds(s0, KEY_COLS), :] for s0 in starts], axis=0)
        qs = jnp.concatenate([jnp.where(m, qf[q0:q1], 0.0) for m in head_masks[nq]],
                             axis=0).astype(BF16)
        s = lax.dot_general(qs, kg, (((1,), (1,)), ((), ())), preferred_element_type=F32)
        biased = []
        for h in range(HEADS_PER_CHUNK):
            bias = jnp.concatenate(
                [tab_ref[c * HEADS_PER_CHUNK + h, (WIN_H - 1) - row_off + KEY_ROWS_PER_BLOCK * m, q0:q1, :]
                 for m in range(n_blocks)], axis=1)
            biased.append(s[h * nq:(h + 1) * nq] + bias)
        s = jnp.concatenate(biased, axis=0)
        return (s, jnp.max(s, axis=-1, keepdims=True)), starts

    def finish(geom, c, group, s_and_max, starts):
        qoff = geom[0]
        q0, q1, _ = group
        nq = q1 - q0
        masks = head_masks[nq]
        s, s_max = s_and_max
        p = jnp.exp2(s - s_max)
        l = jnp.sum(p, axis=-1, keepdims=True)
        vg = jnp.concatenate([vbuf[0, c, pl.ds(s0, KEY_COLS), :] for s0 in starts], axis=0)
        pv = jnp.dot(p.astype(BF16), vg, preferred_element_type=F32)
        last = HEADS_PER_CHUNK - 1
        oc = pv[last * nq:]
        lc = jnp.broadcast_to(l[last * nq:], (nq, cw))
        for h in range(last - 1, -1, -1):
            oc = jnp.where(masks[h], pv[h * nq:(h + 1) * nq], oc)
            lc = jnp.where(masks[h], l[h * nq:(h + 1) * nq], lc)
        obuf[c, qoff + q0:qoff + q1, :] = oc * (1.0 / lc)

    def epilogue_pieces(row0, row1):
        tok = slice(row0 * GRID_W, row1 * GRID_W)
        state = {}

        def gate():
            o_part = jnp.concatenate([obuf[c, tok, :] for c in range(n_chunks)], axis=1)
            state["a"] = (o_part * _silu(gate_ref[0, tok, :])).astype(BF16)

        def project(nb):
            state[nb] = jnp.dot(state["a"], wout_ref[:, nb * cw:(nb + 1) * cw], preferred_element_type=F32)

        def normalise():
            h2 = jnp.concatenate([state[nb] for nb in range(n_chunks)], axis=1)
            z = DEEPNORM_ALPHA * x_ref[0, tok, :] + h2
            o_ref[0, tok, :] = _layer_norm(z, g_ref[...], b_ref[...])

        return [gate] + [functools.partial(project, nb) for nb in range(n_chunks)] + [normalise]

    units = [(i, c, g) for i in range(ROWS_PER_STEP) for c in range(n_chunks) for g in QUERY_GROUPS]
    geoms = [row_geometry(i) for i in range(ROWS_PER_STEP)]
    units_per_row = n_chunks * len(QUERY_GROUPS)
    part_ends = np.cumsum(EPILOGUE_ROW_SPLITS)
    assert part_ends[-1] == ROWS_PER_STEP
    pending, epilogue = [], []
    n_finished = 0
    for n_scored in range(len(units) + SCORE_LOOKAHEAD):
        if n_scored < len(units):
            i, c, g = units[n_scored]
            pending.append(((geoms[i], c, g), scores(geoms[i], c, g)))
        if n_scored >= SCORE_LOOKAHEAD:
            done, (s, starts) = pending.pop(0)
            finish(*done, s, starts)
            n_finished += 1
            if epilogue:
                epilogue.pop(0)()
            if n_finished % units_per_row == 0 and n_finished // units_per_row in part_ends:
                row1 = n_finished // units_per_row
                row0 = row1 - EPILOGUE_ROW_SPLITS[list(part_ends).index(row1)]
                epilogue += epilogue_pieces(row0, row1)
    for piece in epilogue:
        piece()


def _bias_table_kernel(y_ref, o_ref):
    q = lax.broadcasted_iota(jnp.int32, (GRID_W, V7X_LANES), 0)
    kcc = lax.broadcasted_iota(jnp.int32, (GRID_W, V7X_LANES), 1) % KEY_COLS
    cs = jnp.clip(q - WIN_W // 2, 0, GRID_W - WIN_W)
    k0 = jnp.zeros_like(q)
    for q0, _, first_key in QUERY_GROUPS:
        k0 = jnp.where(q >= q0, first_key, k0)
    kc = k0 + kcc
    valid = (kc >= cs) & (kc < cs + WIN_W)
    for d in range(o_ref.shape[1]):
        yb = jnp.broadcast_to(y_ref[0, d:d + 1, :], (GRID_W, V7X_LANES))
        t = jnp.zeros((GRID_W, V7X_LANES), F32)
        for q0, _, first_key in QUERY_GROUPS:
            shift = (-(first_key + WIN_W - 1)) % V7X_LANES
            rolled = pltpu.roll(yb, shift, 1, stride=1, stride_axis=0)
            t = jnp.where(q >= q0, rolled, t)
        o_ref[0, d] = jnp.where(valid, t * LOG2_E, MASK_VALUE)


def _bias_table(rpb):
    n_heads = rpb.shape[0]
    n_d = 2 * WIN_H - KEY_ROWS_PER_BLOCK
    padded = jnp.pad(rpb.astype(F32), ((0, 0), (0, 0), (0, KEY_COLS - rpb.shape[2])))
    y = jnp.stack([padded[:, d:d + KEY_ROWS_PER_BLOCK].reshape(n_heads, V7X_LANES) for d in range(n_d)], axis=1)
    return pl.pallas_call(
        _bias_table_kernel,
        out_shape=jax.ShapeDtypeStruct((n_heads, n_d, GRID_W, V7X_LANES), F32),
        grid=(n_heads,),
        in_specs=[pl.BlockSpec((1, n_d, V7X_LANES), lambda h: (h, 0, 0))],
        out_specs=pl.BlockSpec((1, n_d, GRID_W, V7X_LANES), lambda h: (h, 0, 0, 0)),
        compiler_params=pltpu.CompilerParams(dimension_semantics=("arbitrary",)),
        name="bias_table",
    )(y)


def _na_layer(q, k, v, gate, x, tab, w_out, ln_g, ln_b):
    bsz, seq, d = x.shape
    rows = seq // GRID_W
    n_chunks, cw = q.shape[1], q.shape[3]
    main = ROWS_PER_STEP * GRID_W
    window = (ROWS_PER_STEP + 2 * HALO_ROWS) * GRID_W
    assert rows % ROWS_PER_STEP == 0 and rows * GRID_W >= window
    main_spec = pl.BlockSpec((1, n_chunks, main, cw), lambda b, j: (b, 0, j, 0))
    kv_spec = pl.BlockSpec((pl.Element(1), pl.Element(n_chunks), pl.Element(window), pl.Element(cw)),
                           lambda b, j: (b, 0, _kv_window_start(j, rows) * GRID_W, 0))
    tok_spec = pl.BlockSpec((1, main, d), lambda b, j: (b, j, 0))
    kernel = functools.partial(_na_layer_kernel, rows=rows)
    tile_f32 = _nbytes((main, d), F32)
    vmem_limit = _vmem_limit(
        resident=_nbytes(tab.shape, tab.dtype) + _nbytes(w_out.shape, w_out.dtype),
        pipelined=_nbytes((main, d), BF16) + 2 * _nbytes((window, d), BF16) + 3 * tile_f32,
        scratch=tile_f32,
        temporaries=4 * tile_f32)
    return pl.pallas_call(
        kernel,
        out_shape=jax.ShapeDtypeStruct(x.shape, F32),
        grid=(bsz, rows // ROWS_PER_STEP),
        in_specs=[main_spec, kv_spec, kv_spec, tok_spec, tok_spec, _resident(tab.shape), _resident(w_out.shape),
                  _resident(ln_g.shape), _resident(ln_b.shape)],
        out_specs=tok_spec,
        scratch_shapes=[pltpu.VMEM((n_chunks, main, cw), F32)],
        compiler_params=pltpu.CompilerParams(
            dimension_semantics=("arbitrary", "arbitrary"),
            vmem_limit_bytes=vmem_limit),
        name="na_layer",
    )(q, k, v, gate, x, tab, w_out, ln_g, ln_b)


def _trunk(x, p):
    x1 = _pool_layer(x, p["w_in_pool"], p["w_out_pool"], p["ln_g0"], p["ln_b0"])
    q, k, v, gate = _na_inproj(x1, p["w_in_na"])
    return _na_layer(q, k, v, gate, x1, p["tab"], p["w_out_na"], p["ln_g1"], p["ln_b1"])


def kernel(x_prompt, x_sample, w_in_pool, w_grp_pool, scale_pool, w_out_pool, w_in_na, rpb_na, w_out_na, ln_g, ln_b):
    scale = scale_pool[0].reshape(1, -1).astype(F32)
    p = {
        "w_in_pool": _fold_pool_weights(w_in_pool[0], w_grp_pool[0], scale),
        "w_out_pool": w_out_pool[0].astype(BF16),
        "w_in_na": w_in_na[0].astype(BF16),
        "w_out_na": w_out_na[0].astype(BF16),
        "tab": _bias_table(rpb_na[0]),
        "ln_g0": ln_g[0].reshape(1, -1).astype(F32),
        "ln_b0": ln_b[0].reshape(1, -1).astype(F32),
        "ln_g1": ln_g[1].reshape(1, -1).astype(F32),
        "ln_b1": ln_b[1].reshape(1, -1).astype(F32),
    }
    return (_trunk(x_prompt, p), _trunk(x_sample, p))
```
